```python
import math
import jax, jax.numpy as jnp
from jax import lax
import numpy as np

D_MODEL = 1024
BATCH = 4
SEQ = 8192
DEPTH = 2

D_MIX = 2 * D_MODEL
SSD_INNER = D_MIX // 2
SSD_HEAD_DIM = 64
SSD_HEADS = SSD_INNER // SSD_HEAD_DIM
SSD_GROUPS = 2
SSD_HEADS_PER_GROUP = SSD_HEADS // SSD_GROUPS
SSD_STATE = 128
CONV_WIDTH = 4
CONV_DIM = SSD_INNER + 2 * SSD_GROUPS * SSD_STATE
CHUNK = 128
POOL_WINDOWS = (2, 4, 8, 16)
POOL_WIDTH = D_MIX - SSD_INNER
POOL_GROUP_DIM = POOL_WIDTH // len(POOL_WINDOWS)
IN_PROJ_DIM = SSD_INNER + CONV_DIM + SSD_HEADS + POOL_WIDTH
MEM_LEN = 256
XATTN_HEADS = 4
XATTN_HEAD_DIM = D_MODEL // XATTN_HEADS
D_FF = int(math.ceil((8 * D_MODEL / 3) / 256) * 256)
EPS = 1e-6

kernel_name = "hybrid_ssd_pool_xattn_block"


def rmsnorm(x, g):
    xf = x.astype(jnp.float32)
    y = xf * lax.rsqrt(jnp.mean(xf * xf, axis=-1, keepdims=True) + EPS)
    return (y * g.astype(jnp.float32)).astype(x.dtype)


def causal_depthwise_conv(u, w, b):
    k, c = w.shape
    out = lax.conv_general_dilated(
        u, w[:, None, :].astype(u.dtype), window_strides=(1,), padding=[(k - 1, 0)],
        dimension_numbers=("NWC", "WIO", "NWC"), feature_group_count=c)
    return out + b.astype(u.dtype)


def ssd_chunked(xs, dt, a, bm, cm):
    bsz, s, _, p = xs.shape
    nc = s // CHUNK
    g, hg, n = SSD_GROUPS, SSD_HEADS_PER_GROUP, SSD_STATE
    xc = (xs * dt[..., None]).reshape(bsz, nc, CHUNK, g, hg, p)
    adt = (dt * a).reshape(bsz, nc, CHUNK, g, hg).transpose(0, 3, 4, 1, 2)
    bc = bm.reshape(bsz, nc, CHUNK, g, n)
    cc = cm.reshape(bsz, nc, CHUNK, g, n)
    a_cs = jnp.cumsum(adt, axis=-1)
    causal = jnp.tril(jnp.ones((CHUNK, CHUNK), dtype=bool))
    seg = jnp.exp(jnp.where(causal, a_cs[..., :, None] - a_cs[..., None, :], -jnp.inf))
    cb = jnp.einsum("bclgn,bcsgn->bgcls", cc, bc)
    y_diag = jnp.einsum("bghcls,bcsghp->bclghp", cb[:, :, None] * seg, xc)
    decay_states = jnp.exp(a_cs[..., -1:] - a_cs)
    states = jnp.einsum("bcsgn,bghcs,bcsghp->bcghpn", bc, decay_states, xc)
    chunk_decay = jnp.exp(a_cs[..., -1])

    def step(carry, inp):
        st, dec = inp
        return dec[..., None, None] * carry + st, carry

    init = jnp.zeros((bsz, g, hg, p, n), jnp.float32)
    _, prev = lax.scan(step, init, (jnp.moveaxis(states, 1, 0), jnp.moveaxis(chunk_decay, -1, 0)))
    y_off = jnp.einsum("bclgn,cbghpn,bghcl->bclghp", cc, prev, jnp.exp(a_cs))
    return (y_diag + y_off).reshape(bsz, s, SSD_HEADS, p)


def multi_scale_pool(v, pool_w, pool_scale):
    bsz, s, _ = v.shape
    vg = v.astype(jnp.float32).reshape(bsz, s, len(POOL_WINDOWS), POOL_GROUP_DIM)
    cs0 = jnp.concatenate([jnp.zeros_like(vg[:, :1]), jnp.cumsum(vg, axis=1)], axis=1)
    pos1 = jnp.arange(1, s + 1, dtype=jnp.float32)
    outs = []
    for gi, w in enumerate(POOL_WINDOWS):
        c = cs0[:, :, gi]
        lower = jnp.pad(c[:, :s + 1 - w], ((0, 0), (w - 1, 0), (0, 0)))
        count = jnp.minimum(pos1, float(w))[None, :, None]
        outs.append((c[:, 1:] - lower) / count - vg[:, :, gi])
    d = jnp.stack(outs, axis=2)
    out = jnp.einsum("bsgc,gcd->bsgd", d, pool_w.astype(jnp.float32)).reshape(bsz, s, POOL_WIDTH)
    return (out * pool_scale.astype(jnp.float32)).astype(v.dtype)


def hybrid_mixer(u, w_in, conv_w, conv_b, dt_bias, a_log, d_skip, ssd_norm, pool_w, pool_scale, w_out):
    bsz, s, _ = u.shape
    proj = u @ w_in
    z, xbc, dt_raw, v = jnp.split(
        proj, [SSD_INNER, SSD_INNER + CONV_DIM, SSD_INNER + CONV_DIM + SSD_HEADS], axis=-1)
    xbc = jax.nn.silu(causal_depthwise_conv(xbc, conv_w, conv_b)).astype(jnp.float32)
    xs, bm, cm = jnp.split(xbc, [SSD_INNER, SSD_INNER + SSD_GROUPS * SSD_STATE], axis=-1)
    xs = xs.reshape(bsz, s, SSD_HEADS, SSD_HEAD_DIM)
    bm = bm.reshape(bsz, s, SSD_GROUPS, SSD_STATE)
    cm = cm.reshape(bsz, s, SSD_GROUPS, SSD_STATE)
    dt = jax.nn.softplus(dt_raw.astype(jnp.float32) + dt_bias.astype(jnp.float32))
    a = -jnp.exp(a_log.astype(jnp.float32))
    y = ssd_chunked(xs, dt, a, bm, cm)
    y = y + d_skip.astype(jnp.float32)[:, None] * xs
    y = y.reshape(bsz, s, SSD_INNER) * jax.nn.silu(z.astype(jnp.float32))
    y_ssd = rmsnorm(y, ssd_norm).astype(u.dtype)
    y_pool = multi_scale_pool(v, pool_w, pool_scale)
    return jnp.concatenate([y_ssd, y_pool], axis=-1) @ w_out


def memory_cross_attention(q_in, m_in, w_q, w_kv, w_o):
    bsz, s, _ = q_in.shape
    q = (q_in @ w_q).reshape(bsz, s, XATTN_HEADS, XATTN_HEAD_DIM)
    k, v = jnp.split(m_in @ w_kv, 2, axis=-1)
    k = k.reshape(bsz, -1, XATTN_HEADS, XATTN_HEAD_DIM)
    v = v.reshape(bsz, -1, XATTN_HEADS, XATTN_HEAD_DIM)
    scores = jnp.einsum("bshd,bmhd->bhsm", q.astype(jnp.float32), k.astype(jnp.float32))
    probs = jax.nn.softmax(scores * (XATTN_HEAD_DIM ** -0.5), axis=-1).astype(v.dtype)
    o = jnp.einsum("bhsm,bmhd->bshd", probs, v).reshape(bsz, s, D_MODEL)
    return o @ w_o


def swiglu(u, w_gate_up, w_down):
    gate, up = jnp.split(u @ w_gate_up, 2, axis=-1)
    return (jax.nn.silu(gate) * up) @ w_down


def setup_inputs(seed: int = 0) -> dict:
    key = jax.random.key(seed)
    ks = iter(jax.random.split(key, 32))
    f32 = jnp.float32

    def nrm(shape, fan_in):
        return jax.random.normal(next(ks), shape, f32) * (fan_in ** -0.5)

    def gain(shape):
        return 1.0 + 0.02 * jax.random.normal(next(ks), shape, f32)

    L = DEPTH
    x = jax.random.normal(next(ks), (BATCH, SEQ, D_MODEL), f32)
    mem = jax.random.normal(next(ks), (BATCH, MEM_LEN, D_MODEL), f32)
    dt0 = jnp.exp(jax.random.uniform(next(ks), (L, SSD_HEADS), f32, math.log(1e-3), math.log(1e-1)))
    dt_bias = dt0 + jnp.log(-jnp.expm1(-dt0))
    a_log = jnp.log(jax.random.uniform(next(ks), (L, SSD_HEADS), f32, 1.0, 16.0))
    return {
        "x": x,
        "mem": mem,
        "mix_norm": gain((L, D_MODEL)),
        "w_in": nrm((L, D_MODEL, IN_PROJ_DIM), D_MODEL),
        "conv_w": nrm((L, CONV_WIDTH, CONV_DIM), CONV_WIDTH),
        "conv_b": 0.02 * jax.random.normal(next(ks), (L, CONV_DIM), f32),
        "dt_bias": dt_bias,
        "a_log": a_log,
        "d_skip": gain((L, SSD_HEADS)),
        "ssd_norm": gain((L, SSD_INNER)),
        "pool_w": nrm((L, len(POOL_WINDOWS), POOL_GROUP_DIM, POOL_GROUP_DIM), POOL_GROUP_DIM),
        "pool_scale": gain((L, POOL_WIDTH)),
        "w_out_mix": nrm((L, D_MIX, D_MODEL), D_MIX),
        "xattn_norm": gain((L, D_MODEL)),
        "mem_norm": gain((L, D_MODEL)),
        "w_q": nrm((L, D_MODEL, D_MODEL), D_MODEL),
        "w_kv": nrm((L, D_MODEL, 2 * D_MODEL), D_MODEL),
        "w_o": nrm((L, D_MODEL, D_MODEL), D_MODEL),
        "ffn_norm": gain((L, D_MODEL)),
        "w_gate_up": nrm((L, D_MODEL, 2 * D_FF), D_MODEL),
        "w_down": nrm((L, D_FF, D_MODEL), D_FF),
        "final_norm": gain((D_MODEL,)),
    }


def reference(x, mem, mix_norm, w_in, conv_w, conv_b, dt_bias, a_log, d_skip, ssd_norm,
              pool_w, pool_scale, w_out_mix, xattn_norm, mem_norm, w_q, w_kv, w_o,
              ffn_norm, w_gate_up, w_down, final_norm):
    h = x
    for l in range(DEPTH):
        h = h + hybrid_mixer(rmsnorm(h, mix_norm[l]), w_in[l], conv_w[l], conv_b[l], dt_bias[l],
                             a_log[l], d_skip[l], ssd_norm[l], pool_w[l], pool_scale[l], w_out_mix[l])
        h = h + memory_cross_attention(rmsnorm(h, xattn_norm[l]), rmsnorm(mem, mem_norm[l]),
                                       w_q[l], w_kv[l], w_o[l])
        h = h + swiglu(rmsnorm(h, ffn_norm[l]), w_gate_up[l], w_down[l])
    return rmsnorm(h, final_norm)
```

```python
import functools

import numpy as np
import jax
import jax.numpy as jnp
from jax import lax
from jax.experimental import pallas as pl
from jax.experimental.pallas import tpu as pltpu

F32 = jnp.float32
BF16 = jnp.bfloat16

D_MODEL = 1024
SSD_INNER = 1024
SSD_HEADS = 16
SSD_HEAD_DIM = 64
SSD_GROUPS = 2
SSD_STATE = 128
GROUP_WIDTH = SSD_INNER // SSD_GROUPS
CONV_WIDTH = 4
CONV_DIM = SSD_INNER + 2 * SSD_GROUPS * SSD_STATE
CHUNK = 128
POOL_WINDOWS = (2, 4, 8, 16)
POOL_WIDTH = 1024
POOL_GROUP_DIM = 256
D_MIX = 2048
DT_LANES = 128
DT_COPIES = 3
IN_COLS = SSD_INNER + CONV_DIM + POOL_WIDTH + DT_LANES
MEM_LEN = 256
XATTN_HEADS = 4
XATTN_HEAD_DIM = 256
D_FF = 2816
FF_COL_CHUNK = 1408
EPS = 1e-6

CONV_HALO = 8
SEQ_BLOCK = 512
FFN_BLOCK = 512
VMEM_LIMIT_BYTES = 56 * 1024 * 1024


def _rms(x, g):
    ms = jnp.mean(x * x, axis=-1, keepdims=True)
    return x * lax.rsqrt(ms + EPS) * g


def _silu(x):
    return x / (1.0 + jnp.exp(-x))


def _softplus(x):
    return jnp.maximum(x, 0.0) + jnp.log1p(jnp.exp(-jnp.abs(x)))


def _dot(a, b):
    return jnp.dot(a, b, preferred_element_type=F32)


def _split3(x):
    hi = x.astype(BF16)
    r1 = x - hi.astype(F32)
    mid = r1.astype(BF16)
    lo = (r1 - mid.astype(F32)).astype(BF16)
    return hi, mid, lo


def _pack3(x, lane):
    hi, mid, lo = _split3(x)
    return jnp.where(lane < SSD_HEADS, hi, jnp.where(lane < 2 * SSD_HEADS, mid, lo))


def _mixer_kernel(h_ref, g_ref, win_ref, convw_ref, convb_ref, dtb_ref, alog_ref, dskip_ref,
                  ssdn_ref, poolw_ref, pscale_ref, wout_ref, tri_ref, e64_ref, e128_ref, band_ref,
                  out_ref,
                  z_s, xbc_s, xbcs_s, dt_s, vf_s, vb_s, ycat_s, state_s, acst_s):
    ts = z_s.shape[0]
    j = pl.program_id(1)

    @pl.when(j == 0)
    def _():
        state_s[...] = jnp.zeros_like(state_s)
        xbc_s[0:CONV_HALO, :] = jnp.zeros((CONV_HALO, CONV_DIM), F32)
        vb_s[0:CHUNK, :] = jnp.zeros((CHUNK, POOL_WIDTH), BF16)

    x = h_ref[0]
    u = _rms(x, g_ref[...]).astype(BF16)
    z_s[...] = _dot(u, win_ref[:, 0:SSD_INNER])
    xbc_s[CONV_HALO:CONV_HALO + ts, :] = _dot(u, win_ref[:, SSD_INNER:SSD_INNER + CONV_DIM])
    v = _dot(u, win_ref[:, SSD_INNER + CONV_DIM:SSD_INNER + CONV_DIM + POOL_WIDTH])
    vf_s[...] = v
    vb_s[CHUNK:CHUNK + ts, :] = v.astype(BF16)
    dt_s[...] = _dot(u, win_ref[:, IN_COLS - DT_LANES:IN_COLS])

    acc = convb_ref[...]
    for k in range(CONV_WIDTH):
        off = CONV_HALO - (CONV_WIDTH - 1) + k
        acc = acc + convw_ref[k:k + 1, :] * xbc_s[off:off + ts, :]
    xbcs_s[...] = _silu(acc)
    xbc_s[0:CONV_HALO, :] = xbc_s[ts:ts + CONV_HALO, :]

    lane = lax.broadcasted_iota(jnp.int32, (CHUNK, CHUNK), 1)
    row = lax.broadcasted_iota(jnp.int32, (CHUNK, CHUNK), 0)
    causal = row >= lane
    a_row = jnp.where(lane[0:1, :] < DT_COPIES * SSD_HEADS, -jnp.exp(alog_ref[...]), 0.0)
    dtb = dtb_ref[...]

    def chunk(c, carry):
        r = pl.multiple_of(c * CHUNK, CHUNK)
        xs = xbcs_s[pl.ds(r, CHUNK), 0:SSD_INNER]
        dt = _softplus(dt_s[pl.ds(r, CHUNK), :] + dtb)
        adt = dt * a_row
        hi, mid, lo = _split3(adt)
        acs = _dot(tri_ref[...], jnp.concatenate([hi, mid, lo], axis=0))
        acst_s[...] = acs.T
        pk_acs = _pack3(acs, lane)
        pk_dt = _pack3(dt, lane)
        acs_x = _dot(pk_acs, e64_ref[...])
        dt_x = _dot(pk_dt, e64_ref[...])
        eacs_x = jnp.exp(acs_x)
        alast_x = acs_x[CHUNK - 1:CHUNK, :]
        xc = xs * dt_x
        xdec = (xc * jnp.exp(alast_x - acs_x)).astype(BF16)
        cdecay_x = eacs_x[CHUNK - 1:CHUNK, :]

        ydiag = []
        yoff = []
        for g in range(SSD_GROUPS):
            bm = xbcs_s[pl.ds(r, CHUNK), SSD_INNER + g * SSD_STATE:SSD_INNER + (g + 1) * SSD_STATE]
            cm = xbcs_s[pl.ds(r, CHUNK),
                        SSD_INNER + (SSD_GROUPS + g) * SSD_STATE:SSD_INNER + (SSD_GROUPS + g + 1) * SSD_STATE]
            bmt = bm.T.astype(BF16)
            cmb = cm.astype(BF16)
            cb = _dot(cmb, bmt)
            st = state_s[g]
            yoff.append(_dot(cmb, st.astype(BF16)))
            new_st = _dot(bmt, xdec[:, g * GROUP_WIDTH:(g + 1) * GROUP_WIDTH])
            state_s[g] = st * cdecay_x[:, g * GROUP_WIDTH:(g + 1) * GROUP_WIDTH] + new_st
            pairs_per_group = SSD_HEADS // SSD_GROUPS // 2
            for pp in range(pairs_per_group):
                p = g * pairs_per_group + pp
                cpair = _dot(pk_acs, e128_ref[:, p * 2 * CHUNK:(p + 1) * 2 * CHUNK])
                ms = []
                for q in range(2):
                    hh = 2 * p + q
                    dmat = cpair[:, q * CHUNK:(q + 1) * CHUNK] - acst_s[hh:hh + 1, :]
                    seg = jnp.exp(jnp.where(causal, dmat, -jnp.inf))
                    ms.append((cb * seg).astype(BF16))
                xp = xc[:, p * CHUNK:(p + 1) * CHUNK]
                rhs = jnp.concatenate(
                    [jnp.where(lane < SSD_HEAD_DIM, xp, 0.0).astype(BF16),
                     jnp.where(lane >= SSD_HEAD_DIM, xp, 0.0).astype(BF16)], axis=0)
                ydiag.append(_dot(jnp.concatenate(ms, axis=1), rhs))
        y = (jnp.concatenate(ydiag, axis=1) + jnp.concatenate(yoff, axis=1) * eacs_x
             + dskip_ref[...] * xs)
        y = y * _silu(z_s[pl.ds(r, CHUNK), :])
        ycat_s[pl.ds(r, CHUNK), 0:SSD_INNER] = _rms(y, ssdn_ref[...]).astype(BF16)

        tpos = (j * ts + r + 1 + lax.broadcasted_iota(jnp.int32, (CHUNK, POOL_GROUP_DIM), 0)).astype(F32)
        for gi, w in enumerate(POOL_WINDOWS):
            cols = slice(gi * POOL_GROUP_DIM, (gi + 1) * POOL_GROUP_DIM)
            vv = vb_s[pl.ds(r, 2 * CHUNK), cols]
            win = _dot(band_ref[gi], vv)
            d = win / jnp.minimum(tpos, float(w)) - vf_s[pl.ds(r, CHUNK), cols]
            yp = _dot(d.astype(BF16), poolw_ref[gi]) * pscale_ref[:, cols]
            ycat_s[pl.ds(r, CHUNK), SSD_INNER + gi * POOL_GROUP_DIM:SSD_INNER + (gi + 1) * POOL_GROUP_DIM] = (
                yp.astype(BF16))
        return carry

    lax.fori_loop(0, ts // CHUNK, chunk, 0)
    vb_s[0:CHUNK, :] = vb_s[ts:ts + CHUNK, :]
    out_ref[0] = x + _dot(ycat_s[...], wout_ref[...])


def _const_spec(shape):
    nd = len(shape)
    return pl.BlockSpec(shape, lambda *_: (0,) * nd, pipeline_mode=pl.Buffered(1))


def _mixer_constants():
    t = np.arange(CHUNK)
    tri = (t[:, None] >= t[None, :]).astype(np.float32)
    tri3 = np.concatenate([tri] * DT_COPIES, axis=1)
    e64 = np.zeros((DT_LANES, SSD_INNER), np.float32)
    e128 = np.zeros((DT_LANES, SSD_HEADS * CHUNK), np.float32)
    for c in range(DT_COPIES):
        for h in range(SSD_HEADS):
            e64[c * SSD_HEADS + h, h * SSD_HEAD_DIM:(h + 1) * SSD_HEAD_DIM] = 1.0
            e128[c * SSD_HEADS + h, h * CHUNK:(h + 1) * CHUNK] = 1.0
    k = np.arange(2 * CHUNK) - CHUNK
    band = np.stack([((k[None, :] <= t[:, None]) & (k[None, :] >= t[:, None] - w + 1)).astype(np.float32)
                     for w in POOL_WINDOWS])
    return (jnp.asarray(tri3, BF16), jnp.asarray(e64, BF16), jnp.asarray(e128, BF16), jnp.asarray(band, BF16))


def _mixer(h, g, w_in, conv_w, conv_b, dt_bias, a_log, d_skip, ssd_norm, pool_w, pool_scale, w_out):
    bsz, s, _ = h.shape
    ts = SEQ_BLOCK
    z_end = SSD_INNER
    xbc_end = z_end + CONV_DIM
    dt_end = xbc_end + SSD_HEADS
    w_dt = w_in[:, xbc_end:dt_end]
    w_dt3 = jnp.concatenate([w_dt] * DT_COPIES + [jnp.zeros((D_MODEL, DT_LANES - DT_COPIES * SSD_HEADS), F32)],
                            axis=1)
    win = jnp.concatenate([w_in[:, :xbc_end], w_in[:, dt_end:], w_dt3], axis=1).astype(BF16)
    pad = jnp.zeros((DT_LANES - DT_COPIES * SSD_HEADS,), F32)
    dtb = jnp.concatenate([dt_bias] * DT_COPIES + [pad]).reshape(1, DT_LANES)
    alog = jnp.concatenate([a_log] * DT_COPIES + [pad]).reshape(1, DT_LANES)
    dskip_x = jnp.repeat(d_skip, SSD_HEAD_DIM).reshape(1, SSD_INNER)
    tri3, e64, e128, band = _mixer_constants()
    row = lambda a: a.reshape(1, -1)
    args = (h, row(g), win, conv_w, row(conv_b), dtb, alog, dskip_x, row(ssd_norm), pool_w.astype(BF16),
            row(pool_scale), w_out.astype(BF16), tri3, e64, e128, band)
    in_specs = [pl.BlockSpec((1, ts, D_MODEL), lambda b, j: (b, j, 0))]
    in_specs += [_const_spec(a.shape) for a in args[1:]]
    return pl.pallas_call(
        _mixer_kernel,
        grid=(bsz, s // ts),
        in_specs=in_specs,
        out_specs=pl.BlockSpec((1, ts, D_MODEL), lambda b, j: (b, j, 0)),
        out_shape=jax.ShapeDtypeStruct(h.shape, F32),
        scratch_shapes=[
            pltpu.VMEM((ts, SSD_INNER), F32),
            pltpu.VMEM((CONV_HALO + ts, CONV_DIM), F32),
            pltpu.VMEM((ts, CONV_DIM), F32),
            pltpu.VMEM((ts, DT_LANES), F32),
            pltpu.VMEM((ts, POOL_WIDTH), F32),
            pltpu.VMEM((CHUNK + ts, POOL_WIDTH), BF16),
            pltpu.VMEM((ts, D_MIX), BF16),
            pltpu.VMEM((SSD_GROUPS, SSD_STATE, GROUP_WIDTH), F32),
            pltpu.VMEM((CHUNK, CHUNK), F32),
        ],
        compiler_params=pltpu.CompilerParams(
            dimension_semantics=("parallel", "arbitrary"), vmem_limit_bytes=VMEM_LIMIT_BYTES),
        name="mixer",
    )(*args)


def _kv_kernel(mem_ref, g_ref, wkv_ref, kt_ref, v_ref):
    mn = _rms(mem_ref[0], g_ref[...]).astype(BF16)
    kv = _dot(mn, wkv_ref[...])
    kt_ref[0] = kv[:, 0:D_MODEL].T.astype(BF16)
    v_ref[0] = kv[:, D_MODEL:2 * D_MODEL].astype(BF16)


def _kv(mem, g, w_kv):
    bsz = mem.shape[0]
    return pl.pallas_call(
        _kv_kernel,
        grid=(bsz,),
        in_specs=[pl.BlockSpec((1, MEM_LEN, D_MODEL), lambda b: (b, 0, 0)),
                  _const_spec((1, D_MODEL)), _const_spec((D_MODEL, 2 * D_MODEL))],
        out_specs=[pl.BlockSpec((1, D_MODEL, MEM_LEN), lambda b: (b, 0, 0)),
                   pl.BlockSpec((1, MEM_LEN, D_MODEL), lambda b: (b, 0, 0))],
        out_shape=[jax.ShapeDtypeStruct((bsz, D_MODEL, MEM_LEN), BF16),
                   jax.ShapeDtypeStruct((bsz, MEM_LEN, D_MODEL), BF16)],
        compiler_params=pltpu.CompilerParams(
            dimension_semantics=("parallel",), vmem_limit_bytes=VMEM_LIMIT_BYTES),
        name="memory_kv",
    )(mem, g.reshape(1, -1), w_kv.astype(BF16))


def _xattn_kernel(h_ref, g_ref, wq_ref, kt_ref, v_ref, wo_ref, out_ref, o_s):
    x = h_ref[0]
    u = _rms(x, g_ref[...]).astype(BF16)
    q = (_dot(u, wq_ref[...]) * (XATTN_HEAD_DIM ** -0.5)).astype(BF16)
    for hd in range(XATTN_HEADS):
        cols = slice(hd * XATTN_HEAD_DIM, (hd + 1) * XATTN_HEAD_DIM)
        s = _dot(q[:, cols], kt_ref[0, cols, :])
        p = jnp.exp(s - jnp.max(s, axis=-1, keepdims=True))
        den = jnp.sum(p, axis=-1, keepdims=True)
        o = _dot(p.astype(BF16), v_ref[0, :, cols])
        o_s[:, cols] = (o / den).astype(BF16)
    out_ref[0] = x + _dot(o_s[...], wo_ref[...])


def _xattn(h, g, w_q, kt, v, w_o):
    bsz, s, _ = h.shape
    tq = SEQ_BLOCK
    return pl.pallas_call(
        _xattn_kernel,
        grid=(bsz, s // tq),
        in_specs=[pl.BlockSpec((1, tq, D_MODEL), lambda b, j: (b, j, 0)),
                  _const_spec((1, D_MODEL)), _const_spec((D_MODEL, D_MODEL)),
                  pl.BlockSpec((1, D_MODEL, MEM_LEN), lambda b, j: (b, 0, 0)),
                  pl.BlockSpec((1, MEM_LEN, D_MODEL), lambda b, j: (b, 0, 0)),
                  _const_spec((D_MODEL, D_MODEL))],
        out_specs=pl.BlockSpec((1, tq, D_MODEL), lambda b, j: (b, j, 0)),
        out_shape=jax.ShapeDtypeStruct(h.shape, F32),
        scratch_shapes=[pltpu.VMEM((tq, D_MODEL), BF16)],
        compiler_params=pltpu.CompilerParams(
            dimension_semantics=("parallel", "parallel"), vmem_limit_bytes=VMEM_LIMIT_BYTES),
        name="xattn",
    )(h, g.reshape(1, -1), w_q.astype(BF16), kt, v, w_o.astype(BF16))


def _ffn_kernel(h_ref, g_ref, wgu_ref, wd_ref, gf_ref, out_ref, a_s, *, final_norm):
    x = h_ref[...]
    u = _rms(x, g_ref[...]).astype(BF16)
    for ci in range(D_FF // FF_COL_CHUNK):
        lo = ci * FF_COL_CHUNK
        gate = _dot(u, wgu_ref[:, lo:lo + FF_COL_CHUNK])
        up = _dot(u, wgu_ref[:, D_FF + lo:D_FF + lo + FF_COL_CHUNK])
        a_s[:, lo:lo + FF_COL_CHUNK] = (_silu(gate) * up).astype(BF16)
    y = x + _dot(a_s[...], wd_ref[...])
    if final_norm:
        y = _rms(y, gf_ref[...])
    out_ref[...] = y


def _ffn(h2, g, w_gate_up, w_down, g_final, final_norm):
    t = h2.shape[0]
    tm = FFN_BLOCK
    return pl.pallas_call(
        functools.partial(_ffn_kernel, final_norm=final_norm),
        grid=(t // tm,),
        in_specs=[pl.BlockSpec((tm, D_MODEL), lambda i: (i, 0)),
                  _const_spec((1, D_MODEL)), _const_spec((D_MODEL, 2 * D_FF)),
                  _const_spec((D_FF, D_MODEL)), _const_spec((1, D_MODEL))],
        out_specs=pl.BlockSpec((tm, D_MODEL), lambda i: (i, 0)),
        out_shape=jax.ShapeDtypeStruct(h2.shape, F32),
        scratch_shapes=[pltpu.VMEM((tm, D_FF), BF16)],
        compiler_params=pltpu.CompilerParams(
            dimension_semantics=("parallel",), vmem_limit_bytes=VMEM_LIMIT_BYTES),
        name="swiglu_final" if final_norm else "swiglu",
    )(h2, g.reshape(1, -1), w_gate_up.astype(BF16), w_down.astype(BF16), g_final.reshape(1, -1))


def kernel(x, mem, mix_norm, w_in, conv_w, conv_b, dt_bias, a_log, d_skip, ssd_norm, pool_w, pool_scale,
           w_out_mix, xattn_norm, mem_norm, w_q, w_kv, w_o, ffn_norm, w_gate_up, w_down, final_norm):
    bsz, s, d = x.shape
    depth = w_in.shape[0]
    h = x
    for l in range(depth):
        h = _mixer(h, mix_norm[l], w_in[l], conv_w[l], conv_b[l], dt_bias[l], a_log[l], d_skip[l],
                   ssd_norm[l], pool_w[l], pool_scale[l], w_out_mix[l])
        kt, v = _kv(mem, mem_norm[l], w_kv[l])
        h = _xattn(h, xattn_norm[l], w_q[l], kt, v, w_o[l])
        h = _ffn(h.reshape(bsz * s, d), ffn_norm[l], w_gate_up[l], w_down[l], final_norm,
                 final_norm=(l == depth - 1)).reshape(bsz, s, d)
    return h
```

```python
import functools

import numpy as np
import jax
import jax.numpy as jnp
from jax import lax
from jax.experimental import pallas as pl
from jax.experimental.pallas import tpu as pltpu

F32 = jnp.float32
BF16 = jnp.bfloat16

D_MODEL = 1024
SSD_INNER = 1024
SSD_HEADS = 16
SSD_HEAD_DIM = 64
SSD_GROUPS = 2
SSD_STATE = 128
GROUP_WIDTH = SSD_INNER // SSD_GROUPS
CONV_WIDTH = 4
CONV_DIM = SSD_INNER + 2 * SSD_GROUPS * SSD_STATE
CHUNK = 128
POOL_WINDOWS = (2, 4, 8, 16)
POOL_WIDTH = 1024
POOL_GROUP_DIM = 256
D_MIX = 2048
DT_LANES = 128
DT_COPIES = 3
IN_COLS = SSD_INNER + CONV_DIM + POOL_WIDTH + DT_LANES
MEM_LEN = 256
XATTN_HEADS = 4
XATTN_HEAD_DIM = 256
D_FF = 2816
FF_COL_CHUNK = 1408
EPS = 1e-6

CONV_HALO = 8
CONV_COLS = 256
SEQ_BLOCK = 512
FFN_BLOCK = 512
VMEM_LIMIT_BYTES = 56 * 1024 * 1024


def _rms(x, g):
    ms = jnp.mean(x * x, axis=-1, keepdims=True)
    return x * lax.rsqrt(ms + EPS) * g


def _silu(x):
    return x / (1.0 + jnp.exp(-x))


def _softplus(x):
    return jnp.maximum(x, 0.0) + jnp.log1p(jnp.exp(-jnp.abs(x)))


def _dot(a, b):
    return jnp.dot(a, b, preferred_element_type=F32)


def _split3_f32(x):
    hi = x.astype(BF16).astype(F32)
    r1 = x - hi
    mid = r1.astype(BF16).astype(F32)
    return hi, mid, r1 - mid


def _pack3(x, lane):
    hi, mid, lo = _split3_f32(x)
    return jnp.where(lane < SSD_HEADS, hi, jnp.where(lane < 2 * SSD_HEADS, mid, lo)).astype(BF16)


def _mixer_kernel(h_ref, g_ref, win_ref, convw_ref, convb_ref, dtb_ref, alog_ref, dskip_ref,
                  ssdn_ref, poolw_ref, pscale_ref, wout_ref, tri_ref, e64_ref, e128_ref, band_ref,
                  out_ref,
                  z_s, halo_s, xbcs_s, vf_s, vb_s, state_s, acst_s, pk_s, eacs_s, xdec_s, xlo_s, xhi_s,
                  bmt_s, cmb_s, cb_s, y_s, yoff_s, ycat_s):
    ts = z_s.shape[0]
    nch = ts // CHUNK
    j = pl.program_id(1)

    @pl.when(j == 0)
    def _():
        state_s[...] = jnp.zeros_like(state_s)
        halo_s[...] = jnp.zeros_like(halo_s)
        vb_s[0:CHUNK, :] = jnp.zeros((CHUNK, POOL_WIDTH), BF16)

    def rows(c):
        return slice(c * CHUNK, (c + 1) * CHUNK)

    x = h_ref[0]
    u = _rms(x, g_ref[...]).astype(BF16)

    v_off = SSD_INNER + CONV_DIM

    def z_piece(i):
        z_s[:, i * CONV_COLS:(i + 1) * CONV_COLS] = _dot(u, win_ref[:, i * CONV_COLS:(i + 1) * CONV_COLS])

    def v_piece(i):
        v = _dot(u, win_ref[:, v_off + i * CONV_COLS:v_off + (i + 1) * CONV_COLS])
        vf_s[:, i * CONV_COLS:(i + 1) * CONV_COLS] = v
        vb_s[CHUNK:CHUNK + ts, i * CONV_COLS:(i + 1) * CONV_COLS] = v.astype(BF16)

    def conv_group(cg):
        cols = slice(cg * CONV_COLS, (cg + 1) * CONV_COLS)
        res = _dot(u, win_ref[:, SSD_INNER + cg * CONV_COLS:SSD_INNER + (cg + 1) * CONV_COLS])
        ext = jnp.concatenate([halo_s[:, cols], res], axis=0)
        acc = convb_ref[:, cols] + convw_ref[CONV_WIDTH - 1:CONV_WIDTH, cols] * res
        for k in range(CONV_WIDTH - 1):
            shifted = pltpu.roll(ext, CONV_WIDTH - 1 - k, axis=0)[CONV_HALO:, :]
            acc = acc + convw_ref[k:k + 1, cols] * shifted
        xbcs_s[:, cols] = _silu(acc)
        halo_s[:, cols] = res[ts - CONV_HALO:ts, :]

    lane = lax.broadcasted_iota(jnp.int32, (CHUNK, CHUNK), 1)
    row = lax.broadcasted_iota(jnp.int32, (CHUNK, CHUNK), 0)
    causal = row >= lane
    lane_b = lax.broadcasted_iota(jnp.int32, (ts, DT_LANES), 1)
    a_row = jnp.where(lane[0:1, :] < DT_COPIES * SSD_HEADS, -jnp.exp(alog_ref[...]), 0.0)

    dt = _softplus(_dot(u, win_ref[:, IN_COLS - DT_LANES:IN_COLS]) + dtb_ref[...])
    adt_parts = _split3_f32(dt * a_row)
    conv_group(0)
    acs_c = []
    for c in range(nch):
        stacked = jnp.concatenate([p[rows(c)] for p in adt_parts], axis=0).astype(BF16)
        acs_c.append(_dot(tri_ref[...], stacked))
    conv_group(1)
    for c in range(nch):
        acst_s[c] = acs_c[c].T
    pk_acs = _pack3(jnp.concatenate(acs_c, axis=0), lane_b)
    pk_s[...] = pk_acs
    pk_dt = _pack3(dt, lane_b)
    conv_group(2)
    acs_x = _dot(pk_acs, e64_ref[...])
    dt_x = _dot(pk_dt, e64_ref[...])
    conv_group(3)
    eacs_s[...] = jnp.exp(acs_x)
    z_piece(0)
    lane_x = lax.broadcasted_iota(jnp.int32, (CHUNK, SSD_INNER), 1) % (2 * SSD_HEAD_DIM)
    for c in range(nch):
        r = rows(c)
        xc = xbcs_s[r, 0:SSD_INNER] * dt_x[r]
        alast = acs_x[(c + 1) * CHUNK - 1:(c + 1) * CHUNK, :]
        xdec_s[r, :] = (xc * jnp.exp(alast - acs_x[r])).astype(BF16)
        xlo_s[r, :] = jnp.where(lane_x < SSD_HEAD_DIM, xc, 0.0).astype(BF16)
        xhi_s[r, :] = jnp.where(lane_x >= SSD_HEAD_DIM, xc, 0.0).astype(BF16)
        if c == 1:
            conv_group(4)
    conv_group(5)
    z_piece(1)
    for c in range(nch):
        for g in range(SSD_GROUPS):
            b_cols = slice(SSD_INNER + g * SSD_STATE, SSD_INNER + (g + 1) * SSD_STATE)
            c_cols = slice(SSD_INNER + (SSD_GROUPS + g) * SSD_STATE, SSD_INNER + (SSD_GROUPS + g + 1) * SSD_STATE)
            bmt_s[c, g] = xbcs_s[rows(c), b_cols].T.astype(BF16)
            cmb_s[c, g] = xbcs_s[rows(c), c_cols].astype(BF16)
    z_piece(2)

    pairs_per_group = SSD_HEADS // SSD_GROUPS // 2
    side = [functools.partial(z_piece, 3)] + [functools.partial(v_piece, i) for i in range(SSD_INNER // CONV_COLS)]
    for p in range(SSD_HEADS // 2):
        g = p // pairs_per_group
        pcols = slice(p * 2 * SSD_HEAD_DIM, (p + 1) * 2 * SSD_HEAD_DIM)
        if p % pairs_per_group == 0:
            for c in range(nch):
                cb_s[c] = _dot(cmb_s[c, g], bmt_s[c, g])
        cpair = _dot(pk_s[...], e128_ref[:, p * 2 * CHUNK:(p + 1) * 2 * CHUNK])
        for c in range(nch):
            ms = []
            for q in range(2):
                hh = 2 * p + q
                dmat = cpair[rows(c), q * CHUNK:(q + 1) * CHUNK] - acst_s[c, hh:hh + 1, :]
                seg = jnp.exp(jnp.where(causal, dmat, -jnp.inf))
                ms.append((cb_s[c] * seg).astype(BF16))
            rhs = jnp.concatenate([xlo_s[rows(c), pcols], xhi_s[rows(c), pcols]], axis=0)
            y_s[rows(c), pcols] = _dot(jnp.concatenate(ms, axis=1), rhs)
        if p < len(side):
            side[p]()

    new_st = {}
    for c in range(nch):
        for g in range(SSD_GROUPS):
            gcols = slice(g * GROUP_WIDTH, (g + 1) * GROUP_WIDTH)
            new_st[c, g] = _dot(bmt_s[c, g], xdec_s[rows(c), gcols])
    for c in range(nch):
        for g in range(SSD_GROUPS):
            gcols = slice(g * GROUP_WIDTH, (g + 1) * GROUP_WIDTH)
            st = state_s[g]
            yoff_s[rows(c), gcols] = _dot(cmb_s[c, g], st.astype(BF16))
            state_s[g] = st * eacs_s[(c + 1) * CHUNK - 1:(c + 1) * CHUNK, gcols] + new_st[c, g]

    tpos0 = j * ts + 1 + lax.broadcasted_iota(jnp.int32, (CHUNK, POOL_GROUP_DIM), 0)
    for gi, w in enumerate(POOL_WINDOWS):
        cols = slice(gi * POOL_GROUP_DIM, (gi + 1) * POOL_GROUP_DIM)
        ds = []
        for c in range(nch):
            win = _dot(band_ref[gi], vb_s[c * CHUNK:(c + 2) * CHUNK, cols])
            cnt = jnp.minimum((tpos0 + c * CHUNK).astype(F32), float(w))
            ds.append((win / cnt - vf_s[rows(c), cols]).astype(BF16))
        yp = _dot(jnp.concatenate(ds, axis=0), poolw_ref[gi]) * pscale_ref[:, cols]
        ycat_s[:, SSD_INNER + gi * POOL_GROUP_DIM:SSD_INNER + (gi + 1) * POOL_GROUP_DIM] = yp.astype(BF16)
    vb_s[0:CHUNK, :] = vb_s[ts:ts + CHUNK, :]

    for c in range(nch):
        r = rows(c)
        y = y_s[r, :] + yoff_s[r, :] * eacs_s[r, :] + dskip_ref[...] * xbcs_s[r, 0:SSD_INNER]
        y = y * _silu(z_s[r, :])
        ycat_s[r, 0:SSD_INNER] = _rms(y, ssdn_ref[...]).astype(BF16)
        out_ref[0, r, :] = x[r] + _dot(ycat_s[r, :], wout_ref[...])


def _const_spec(shape):
    nd = len(shape)
    return pl.BlockSpec(shape, lambda *_: (0,) * nd, pipeline_mode=pl.Buffered(1))


def _mixer_constants():
    t = np.arange(CHUNK)
    tri = (t[:, None] >= t[None, :]).astype(np.float32)
    tri3 = np.concatenate([tri] * DT_COPIES, axis=1)
    e64 = np.zeros((DT_LANES, SSD_INNER), np.float32)
    e128 = np.zeros((DT_LANES, SSD_HEADS * CHUNK), np.float32)
    for c in range(DT_COPIES):
        for h in range(SSD_HEADS):
            e64[c * SSD_HEADS + h, h * SSD_HEAD_DIM:(h + 1) * SSD_HEAD_DIM] = 1.0
            e128[c * SSD_HEADS + h, h * CHUNK:(h + 1) * CHUNK] = 1.0
    k = np.arange(2 * CHUNK) - CHUNK
    band = np.stack([((k[None, :] <= t[:, None]) & (k[None, :] >= t[:, None] - w + 1)).astype(np.float32)
                     for w in POOL_WINDOWS])
    return (jnp.asarray(tri3, BF16), jnp.asarray(e64, BF16), jnp.asarray(e128, BF16), jnp.asarray(band, BF16))


def _mixer(h, g, w_in, conv_w, conv_b, dt_bias, a_log, d_skip, ssd_norm, pool_w, pool_scale, w_out):
    bsz, s, _ = h.shape
    ts = SEQ_BLOCK
    nch = ts // CHUNK
    z_end = SSD_INNER
    xbc_end = z_end + CONV_DIM
    dt_end = xbc_end + SSD_HEADS
    w_dt = w_in[:, xbc_end:dt_end]
    w_dt3 = jnp.concatenate([w_dt] * DT_COPIES + [jnp.zeros((D_MODEL, DT_LANES - DT_COPIES * SSD_HEADS), F32)],
                            axis=1)
    win = jnp.concatenate([w_in[:, :xbc_end], w_in[:, dt_end:], w_dt3], axis=1).astype(BF16)
    pad = jnp.zeros((DT_LANES - DT_COPIES * SSD_HEADS,), F32)
    dtb = jnp.concatenate([dt_bias] * DT_COPIES + [pad]).reshape(1, DT_LANES)
    alog = jnp.concatenate([a_log] * DT_COPIES + [pad]).reshape(1, DT_LANES)
    dskip_x = jnp.repeat(d_skip, SSD_HEAD_DIM).reshape(1, SSD_INNER)
    tri3, e64, e128, band = _mixer_constants()
    row = lambda a: a.reshape(1, -1)
    args = (h, row(g), win, conv_w, row(conv_b), dtb, alog, dskip_x, row(ssd_norm), pool_w.astype(BF16),
            row(pool_scale), w_out.astype(BF16), tri3, e64, e128, band)
    in_specs = [pl.BlockSpec((1, ts, D_MODEL), lambda b, j: (b, j, 0))]
    in_specs += [_const_spec(a.shape) for a in args[1:]]
    return pl.pallas_call(
        _mixer_kernel,
        grid=(bsz, s // ts),
        in_specs=in_specs,
        out_specs=pl.BlockSpec((1, ts, D_MODEL), lambda b, j: (b, j, 0)),
        out_shape=jax.ShapeDtypeStruct(h.shape, F32),
        scratch_shapes=[
            pltpu.VMEM((ts, SSD_INNER), F32),
            pltpu.VMEM((CONV_HALO, CONV_DIM), F32),
            pltpu.VMEM((ts, CONV_DIM), F32),
            pltpu.VMEM((ts, POOL_WIDTH), F32),
            pltpu.VMEM((CHUNK + ts, POOL_WIDTH), BF16),
            pltpu.VMEM((SSD_GROUPS, SSD_STATE, GROUP_WIDTH), F32),
            pltpu.VMEM((nch, CHUNK, CHUNK), F32),
            pltpu.VMEM((ts, DT_LANES), BF16),
            pltpu.VMEM((ts, SSD_INNER), F32),
            pltpu.VMEM((ts, SSD_INNER), BF16),
            pltpu.VMEM((ts, SSD_INNER), BF16),
            pltpu.VMEM((ts, SSD_INNER), BF16),
            pltpu.VMEM((nch, SSD_GROUPS, SSD_STATE, CHUNK), BF16),
            pltpu.VMEM((nch, SSD_GROUPS, CHUNK, SSD_STATE), BF16),
            pltpu.VMEM((nch, CHUNK, CHUNK), F32),
            pltpu.VMEM((ts, SSD_INNER), F32),
            pltpu.VMEM((ts, SSD_INNER), F32),
            pltpu.VMEM((ts, D_MIX), BF16),
        ],
        compiler_params=pltpu.CompilerParams(
            dimension_semantics=("parallel", "arbitrary"), vmem_limit_bytes=VMEM_LIMIT_BYTES),
        name="mixer",
    )(*args)


def _kv_kernel(mem_ref, g_ref, wkv_ref, kt_ref, v_ref):
    mn = _rms(mem_ref[0], g_ref[...]).astype(BF16)
    kv = _dot(mn, wkv_ref[...])
    kt_ref[0] = kv[:, 0:D_MODEL].T.astype(BF16)
    v_ref[0] = kv[:, D_MODEL:2 * D_MODEL].astype(BF16)


def _kv(mem, g, w_kv):
    bsz = mem.shape[0]
    return pl.pallas_call(
        _kv_kernel,
        grid=(bsz,),
        in_specs=[pl.BlockSpec((1, MEM_LEN, D_MODEL), lambda b: (b, 0, 0)),
                  _const_spec((1, D_MODEL)), _const_spec((D_MODEL, 2 * D_MODEL))],
        out_specs=[pl.BlockSpec((1, D_MODEL, MEM_LEN), lambda b: (b, 0, 0)),
                   pl.BlockSpec((1, MEM_LEN, D_MODEL), lambda b: (b, 0, 0))],
        out_shape=[jax.ShapeDtypeStruct((bsz, D_MODEL, MEM_LEN), BF16),
                   jax.ShapeDtypeStruct((bsz, MEM_LEN, D_MODEL), BF16)],
        compiler_params=pltpu.CompilerParams(
            dimension_semantics=("parallel",), vmem_limit_bytes=VMEM_LIMIT_BYTES),
        name="memory_kv",
    )(mem, g.reshape(1, -1), w_kv.astype(BF16))


def _xattn_kernel(h_ref, g_ref, wq_ref, kt_ref, v_ref, wo_ref, out_ref, o_s):
    x = h_ref[0]
    u = _rms(x, g_ref[...]).astype(BF16)
    q = (_dot(u, wq_ref[...]) * (XATTN_HEAD_DIM ** -0.5)).astype(BF16)
    for hd in range(XATTN_HEADS):
        cols = slice(hd * XATTN_HEAD_DIM, (hd + 1) * XATTN_HEAD_DIM)
        s = _dot(q[:, cols], kt_ref[0, cols, :])
        p = jnp.exp(s - jnp.max(s, axis=-1, keepdims=True))
        den = jnp.sum(p, axis=-1, keepdims=True)
        o = _dot(p.astype(BF16), v_ref[0, :, cols])
        o_s[:, cols] = (o / den).astype(BF16)
    out_ref[0] = x + _dot(o_s[...], wo_ref[...])


def _xattn(h, g, w_q, kt, v, w_o):
    bsz, s, _ = h.shape
    tq = SEQ_BLOCK
    return pl.pallas_call(
        _xattn_kernel,
        grid=(bsz, s // tq),
        in_specs=[pl.BlockSpec((1, tq, D_MODEL), lambda b, j: (b, j, 0)),
                  _const_spec((1, D_MODEL)), _const_spec((D_MODEL, D_MODEL)),
                  pl.BlockSpec((1, D_MODEL, MEM_LEN), lambda b, j: (b, 0, 0)),
                  pl.BlockSpec((1, MEM_LEN, D_MODEL), lambda b, j: (b, 0, 0)),
                  _const_spec((D_MODEL, D_MODEL))],
        out_specs=pl.BlockSpec((1, tq, D_MODEL), lambda b, j: (b, j, 0)),
        out_shape=jax.ShapeDtypeStruct(h.shape, F32),
        scratch_shapes=[pltpu.VMEM((tq, D_MODEL), BF16)],
        compiler_params=pltpu.CompilerParams(
            dimension_semantics=("parallel", "parallel"), vmem_limit_bytes=VMEM_LIMIT_BYTES),
        name="xattn",
    )(h, g.reshape(1, -1), w_q.astype(BF16), kt, v, w_o.astype(BF16))


def _ffn_kernel(h_ref, g_ref, wgu_ref, wd_ref, gf_ref, out_ref, a_s, *, final_norm):
    x = h_ref[...]
    u = _rms(x, g_ref[...]).astype(BF16)
    for ci in range(D_FF // FF_COL_CHUNK):
        lo = ci * FF_COL_CHUNK
        gate = _dot(u, wgu_ref[:, lo:lo + FF_COL_CHUNK])
        up = _dot(u, wgu_ref[:, D_FF + lo:D_FF + lo + FF_COL_CHUNK])
        a_s[:, lo:lo + FF_COL_CHUNK] = (_silu(gate) * up).astype(BF16)
    y = x + _dot(a_s[...], wd_ref[...])
    if final_norm:
        y = _rms(y, gf_ref[...])
    out_ref[...] = y


def _ffn(h2, g, w_gate_up, w_down, g_final, final_norm):
    t = h2.shape[0]
    tm = FFN_BLOCK
    return pl.pallas_call(
        functools.partial(_ffn_kernel, final_norm=final_norm),
        grid=(t // tm,),
        in_specs=[pl.BlockSpec((tm, D_MODEL), lambda i: (i, 0)),
                  _const_spec((1, D_MODEL)), _const_spec((D_MODEL, 2 * D_FF)),
                  _const_spec((D_FF, D_MODEL)), _const_spec((1, D_MODEL))],
        out_specs=pl.BlockSpec((tm, D_MODEL), lambda i: (i, 0)),
        out_shape=jax.ShapeDtypeStruct(h2.shape, F32),
        scratch_shapes=[pltpu.VMEM((tm, D_FF), BF16)],
        compiler_params=pltpu.CompilerParams(
            dimension_semantics=("parallel",), vmem_limit_bytes=VMEM_LIMIT_BYTES),
        name="swiglu_final" if final_norm else "swiglu",
    )(h2, g.reshape(1, -1), w_gate_up.astype(BF16), w_down.astype(BF16), g_final.reshape(1, -1))


def kernel(x, mem, mix_norm, w_in, conv_w, conv_b, dt_bias, a_log, d_skip, ssd_norm, pool_w, pool_scale,
           w_out_mix, xattn_norm, mem_norm, w_q, w_kv, w_o, ffn_norm, w_gate_up, w_down, final_norm):
    bsz, s, d = x.shape
    depth = w_in.shape[0]
    h = x
    for l in range(depth):
        h = _mixer(h, mix_norm[l], w_in[l], conv_w[l], conv_b[l], dt_bias[l], a_log[l], d_skip[l],
                   ssd_norm[l], pool_w[l], pool_scale[l], w_out_mix[l])
        kt, v = _kv(mem, mem_norm[l], w_kv[l])
        h = _xattn(h, xattn_norm[l], w_q[l], kt, v, w_o[l])
        h = _ffn(h.reshape(bsz * s, d), ffn_norm[l], w_gate_up[l], w_down[l], final_norm,
                 final_norm=(l == depth - 1)).reshape(bsz, s, d)
    return h
```

```python
import functools

import numpy as np
import jax
import jax.numpy as jnp
from jax import lax
from jax.experimental import pallas as pl
from jax.experimental.pallas import tpu as pltpu

F32 = jnp.float32
BF16 = jnp.bfloat16

D_MODEL = 1024
SSD_INNER = 1024
SSD_HEADS = 16
SSD_HEAD_DIM = 64
SSD_GROUPS = 2
SSD_STATE = 128
GROUP_WIDTH = SSD_INNER // SSD_GROUPS
CONV_WIDTH = 4
CONV_DIM = SSD_INNER + 2 * SSD_GROUPS * SSD_STATE
CHUNK = 128
POOL_WINDOWS = (2, 4, 8, 16)
POOL_WIDTH = 1024
POOL_GROUP_DIM = 256
D_MIX = 2048
DT_LANES = 128
DT_COPIES = 3
IN_COLS = SSD_INNER + CONV_DIM + POOL_WIDTH + DT_LANES
MEM_LEN = 256
XATTN_HEADS = 4
XATTN_HEAD_DIM = 256
D_FF = 2816
FF_COL_CHUNK = 1408
EPS = 1e-6

CONV_HALO = 8
CONV_COLS = 256
SEQ_BLOCK = 512
FFN_BLOCK = 512
VMEM_LIMIT_BYTES = 56 * 1024 * 1024


def _rms(x, g):
    ms = jnp.mean(x * x, axis=-1, keepdims=True)
    return x * lax.rsqrt(ms + EPS) * g


def _silu(x):
    return x / (1.0 + jnp.exp(-x))


def _softplus(x):
    return jnp.maximum(x, 0.0) + jnp.log1p(jnp.exp(-jnp.abs(x)))


def _dot(a, b):
    return jnp.dot(a, b, preferred_element_type=F32)


def _split3_f32(x):
    hi = x.astype(BF16).astype(F32)
    r1 = x - hi
    mid = r1.astype(BF16).astype(F32)
    return hi, mid, r1 - mid


def _pack3(x, lane):
    hi, mid, lo = _split3_f32(x)
    return jnp.where(lane < SSD_HEADS, hi, jnp.where(lane < 2 * SSD_HEADS, mid, lo)).astype(BF16)


def _mixer_kernel(h_ref, g_ref, win_ref, convw_ref, convb_ref, dtb_ref, alog_ref, dskip_ref,
                  ssdn_ref, poolw_ref, pscale_ref, wout_ref, tri_ref, e64_ref, e128_ref, band_ref,
                  out_ref,
                  z_s, halo_s, xbcs_s, vf_s, vb_s, state_s, acst_s, pk_s, eacs_s, xdec_s, xlo_s, xhi_s,
                  bmt_s, cmb_s, cb_s, y_s, yoff_s, ycat_s):
    ts = z_s.shape[0]
    nch = ts // CHUNK
    j = pl.program_id(1)

    @pl.when(j == 0)
    def _():
        state_s[...] = jnp.zeros_like(state_s)
        halo_s[...] = jnp.zeros_like(halo_s)
        vb_s[0:CHUNK, :] = jnp.zeros((CHUNK, POOL_WIDTH), BF16)

    def rows(c):
        return slice(c * CHUNK, (c + 1) * CHUNK)

    x = h_ref[0]
    u = _rms(x, g_ref[...]).astype(BF16)

    v_off = SSD_INNER + CONV_DIM

    def z_piece(i):
        z_s[:, i * CONV_COLS:(i + 1) * CONV_COLS] = _dot(u, win_ref[:, i * CONV_COLS:(i + 1) * CONV_COLS])

    def v_piece(i):
        v = _dot(u, win_ref[:, v_off + i * CONV_COLS:v_off + (i + 1) * CONV_COLS])
        vf_s[:, i * CONV_COLS:(i + 1) * CONV_COLS] = v
        vb_s[CHUNK:CHUNK + ts, i * CONV_COLS:(i + 1) * CONV_COLS] = v.astype(BF16)

    def conv_group(cg):
        cols = slice(cg * CONV_COLS, (cg + 1) * CONV_COLS)
        res = _dot(u, win_ref[:, SSD_INNER + cg * CONV_COLS:SSD_INNER + (cg + 1) * CONV_COLS])
        w = [convw_ref[k:k + 1, cols] for k in range(CONV_WIDTH)]
        ext = jnp.concatenate([halo_s[:, cols], res], axis=0)
        prev = pltpu.roll(ext, 1, axis=0)
        b = w[1] * ext + w[0] * prev
        a = convb_ref[:, cols] + w[3] * res + w[2] * prev[CONV_HALO:, :]
        xbcs_s[:, cols] = _silu(a + pltpu.roll(b, 2, axis=0)[CONV_HALO:, :])
        halo_s[:, cols] = res[ts - CONV_HALO:ts, :]

    lane = lax.broadcasted_iota(jnp.int32, (CHUNK, CHUNK), 1)
    row = lax.broadcasted_iota(jnp.int32, (CHUNK, CHUNK), 0)
    causal = row >= lane
    lane_b = lax.broadcasted_iota(jnp.int32, (ts, DT_LANES), 1)
    a_row = jnp.where(lane[0:1, :] < DT_COPIES * SSD_HEADS, -jnp.exp(alog_ref[...]), 0.0)

    dt = _softplus(_dot(u, win_ref[:, IN_COLS - DT_LANES:IN_COLS]) + dtb_ref[...])
    adt_parts = _split3_f32(dt * a_row)
    conv_group(0)
    acs_c = []
    for c in range(nch):
        stacked = jnp.concatenate([p[rows(c)] for p in adt_parts], axis=0).astype(BF16)
        acs_c.append(_dot(tri_ref[...], stacked))
    conv_group(1)
    for c in range(nch):
        acst_s[c] = acs_c[c].T
    pk_acs = _pack3(jnp.concatenate(acs_c, axis=0), lane_b)
    pk_s[...] = pk_acs
    pk_dt = _pack3(dt, lane_b)
    conv_group(2)
    acs_x = _dot(pk_acs, e64_ref[...])
    dt_x = _dot(pk_dt, e64_ref[...])
    conv_group(3)
    eacs_s[...] = jnp.exp(acs_x)
    z_piece(0)
    lane_x = lax.broadcasted_iota(jnp.int32, (CHUNK, SSD_INNER), 1) % (2 * SSD_HEAD_DIM)
    for c in range(nch):
        r = rows(c)
        xc = xbcs_s[r, 0:SSD_INNER] * dt_x[r]
        alast = acs_x[(c + 1) * CHUNK - 1:(c + 1) * CHUNK, :]
        xdec_s[r, :] = (xc * jnp.exp(alast - acs_x[r])).astype(BF16)
        xlo_s[r, :] = jnp.where(lane_x < SSD_HEAD_DIM, xc, 0.0).astype(BF16)
        xhi_s[r, :] = jnp.where(lane_x >= SSD_HEAD_DIM, xc, 0.0).astype(BF16)
        if c == 1:
            conv_group(4)
    conv_group(5)
    z_piece(1)
    for c in range(nch):
        for g in range(SSD_GROUPS):
            b_cols = slice(SSD_INNER + g * SSD_STATE, SSD_INNER + (g + 1) * SSD_STATE)
            c_cols = slice(SSD_INNER + (SSD_GROUPS + g) * SSD_STATE, SSD_INNER + (SSD_GROUPS + g + 1) * SSD_STATE)
            bmt_s[c, g] = xbcs_s[rows(c), b_cols].T.astype(BF16)
            cmb_s[c, g] = xbcs_s[rows(c), c_cols].astype(BF16)
    z_piece(2)

    pairs_per_group = SSD_HEADS // SSD_GROUPS // 2
    side = [functools.partial(z_piece, 3)] + [functools.partial(v_piece, i) for i in range(SSD_INNER // CONV_COLS)]
    for p in range(SSD_HEADS // 2):
        g = p // pairs_per_group
        pcols = slice(p * 2 * SSD_HEAD_DIM, (p + 1) * 2 * SSD_HEAD_DIM)
        if p % pairs_per_group == 0:
            for c in range(nch):
                cb_s[c] = _dot(cmb_s[c, g], bmt_s[c, g])
        cpair = _dot(pk_s[...], e128_ref[:, p * 2 * CHUNK:(p + 1) * 2 * CHUNK])
        for c in range(nch):
            ms = []
            for q in range(2):
                hh = 2 * p + q
                dmat = cpair[rows(c), q * CHUNK:(q + 1) * CHUNK] - acst_s[c, hh:hh + 1, :]
                seg = jnp.exp(jnp.where(causal, dmat, -jnp.inf))
                ms.append((cb_s[c] * seg).astype(BF16))
            rhs = jnp.concatenate([xlo_s[rows(c), pcols], xhi_s[rows(c), pcols]], axis=0)
            y_s[rows(c), pcols] = _dot(jnp.concatenate(ms, axis=1), rhs)
        if p < len(side):
            side[p]()

    new_st = {}
    for c in range(nch):
        for g in range(SSD_GROUPS):
            gcols = slice(g * GROUP_WIDTH, (g + 1) * GROUP_WIDTH)
            new_st[c, g] = _dot(bmt_s[c, g], xdec_s[rows(c), gcols])
    for c in range(nch):
        for g in range(SSD_GROUPS):
            gcols = slice(g * GROUP_WIDTH, (g + 1) * GROUP_WIDTH)
            st = state_s[g]
            yoff_s[rows(c), gcols] = _dot(cmb_s[c, g], st.astype(BF16))
            state_s[g] = st * eacs_s[(c + 1) * CHUNK - 1:(c + 1) * CHUNK, gcols] + new_st[c, g]

    tpos0 = j * ts + 1 + lax.broadcasted_iota(jnp.int32, (CHUNK, POOL_GROUP_DIM), 0)
    for gi, w in enumerate(POOL_WINDOWS):
        cols = slice(gi * POOL_GROUP_DIM, (gi + 1) * POOL_GROUP_DIM)
        ds = []
        for c in range(nch):
            win = _dot(band_ref[gi], vb_s[c * CHUNK:(c + 2) * CHUNK, cols])
            cnt = jnp.minimum((tpos0 + c * CHUNK).astype(F32), float(w))
            ds.append((win / cnt - vf_s[rows(c), cols]).astype(BF16))
        yp = _dot(jnp.concatenate(ds, axis=0), poolw_ref[gi]) * pscale_ref[:, cols]
        ycat_s[:, SSD_INNER + gi * POOL_GROUP_DIM:SSD_INNER + (gi + 1) * POOL_GROUP_DIM] = yp.astype(BF16)
    vb_s[0:CHUNK, :] = vb_s[ts:ts + CHUNK, :]

    for c in range(nch):
        r = rows(c)
        y = y_s[r, :] + yoff_s[r, :] * eacs_s[r, :] + dskip_ref[...] * xbcs_s[r, 0:SSD_INNER]
        y = y * _silu(z_s[r, :])
        ycat_s[r, 0:SSD_INNER] = _rms(y, ssdn_ref[...]).astype(BF16)
        out_ref[0, r, :] = x[r] + _dot(ycat_s[r, :], wout_ref[...])


def _const_spec(shape):
    nd = len(shape)
    return pl.BlockSpec(shape, lambda *_: (0,) * nd, pipeline_mode=pl.Buffered(1))


def _layer_spec(shape, layer):
    nd = len(shape)
    return pl.BlockSpec((None,) + tuple(shape[1:]), lambda *_: (layer,) + (0,) * (nd - 1),
                        pipeline_mode=pl.Buffered(1))


def _mixer_constants():
    t = np.arange(CHUNK)
    tri = (t[:, None] >= t[None, :]).astype(np.float32)
    tri3 = np.concatenate([tri] * DT_COPIES, axis=1)
    e64 = np.zeros((DT_LANES, SSD_INNER), np.float32)
    e128 = np.zeros((DT_LANES, SSD_HEADS * CHUNK), np.float32)
    for c in range(DT_COPIES):
        for h in range(SSD_HEADS):
            e64[c * SSD_HEADS + h, h * SSD_HEAD_DIM:(h + 1) * SSD_HEAD_DIM] = 1.0
            e128[c * SSD_HEADS + h, h * CHUNK:(h + 1) * CHUNK] = 1.0
    k = np.arange(2 * CHUNK) - CHUNK
    band = np.stack([((k[None, :] <= t[:, None]) & (k[None, :] >= t[:, None] - w + 1)).astype(np.float32)
                     for w in POOL_WINDOWS])
    return (jnp.asarray(tri3, BF16), jnp.asarray(e64, BF16), jnp.asarray(e128, BF16), jnp.asarray(band, BF16))


def _prep_mixer_params(mix_norm, w_in, conv_w, conv_b, dt_bias, a_log, d_skip, ssd_norm, pool_w, pool_scale,
                       w_out_mix):
    depth = w_in.shape[0]
    xbc_end = SSD_INNER + CONV_DIM
    dt_end = xbc_end + SSD_HEADS
    w_dt = w_in[:, :, xbc_end:dt_end]
    n_pad = DT_LANES - DT_COPIES * SSD_HEADS
    w_dt3 = jnp.concatenate([w_dt] * DT_COPIES + [jnp.zeros((depth, D_MODEL, n_pad), F32)], axis=2)
    win = jnp.concatenate([w_in[:, :, :xbc_end], w_in[:, :, dt_end:], w_dt3], axis=2).astype(BF16)
    pad = jnp.zeros((depth, n_pad), F32)
    row = lambda a: a.reshape(depth, 1, -1)
    dtb = row(jnp.concatenate([dt_bias] * DT_COPIES + [pad], axis=1))
    alog = row(jnp.concatenate([a_log] * DT_COPIES + [pad], axis=1))
    dskip_x = row(jnp.repeat(d_skip, SSD_HEAD_DIM, axis=1))
    return (row(mix_norm), win, conv_w, row(conv_b), dtb, alog, dskip_x, row(ssd_norm), pool_w.astype(BF16),
            row(pool_scale), w_out_mix.astype(BF16))


def _mixer(h, layer, params):
    bsz, s, _ = h.shape
    ts = SEQ_BLOCK
    nch = ts // CHUNK
    consts = _mixer_constants()
    in_specs = [pl.BlockSpec((1, ts, D_MODEL), lambda b, j: (b, j, 0))]
    in_specs += [_layer_spec(a.shape, layer) for a in params]
    in_specs += [_const_spec(a.shape) for a in consts]
    return pl.pallas_call(
        _mixer_kernel,
        grid=(bsz, s // ts),
        in_specs=in_specs,
        out_specs=pl.BlockSpec((1, ts, D_MODEL), lambda b, j: (b, j, 0)),
        out_shape=jax.ShapeDtypeStruct(h.shape, F32),
        scratch_shapes=[
            pltpu.VMEM((ts, SSD_INNER), F32),
            pltpu.VMEM((CONV_HALO, CONV_DIM), F32),
            pltpu.VMEM((ts, CONV_DIM), F32),
            pltpu.VMEM((ts, POOL_WIDTH), F32),
            pltpu.VMEM((CHUNK + ts, POOL_WIDTH), BF16),
            pltpu.VMEM((SSD_GROUPS, SSD_STATE, GROUP_WIDTH), F32),
            pltpu.VMEM((nch, CHUNK, CHUNK), F32),
            pltpu.VMEM((ts, DT_LANES), BF16),
            pltpu.VMEM((ts, SSD_INNER), F32),
            pltpu.VMEM((ts, SSD_INNER), BF16),
            pltpu.VMEM((ts, SSD_INNER), BF16),
            pltpu.VMEM((ts, SSD_INNER), BF16),
            pltpu.VMEM((nch, SSD_GROUPS, SSD_STATE, CHUNK), BF16),
            pltpu.VMEM((nch, SSD_GROUPS, CHUNK, SSD_STATE), BF16),
            pltpu.VMEM((nch, CHUNK, CHUNK), F32),
            pltpu.VMEM((ts, SSD_INNER), F32),
            pltpu.VMEM((ts, SSD_INNER), F32),
            pltpu.VMEM((ts, D_MIX), BF16),
        ],
        compiler_params=pltpu.CompilerParams(
            dimension_semantics=("parallel", "arbitrary"), vmem_limit_bytes=VMEM_LIMIT_BYTES),
        name="mixer",
    )(h, *params, *consts)


def _kv_kernel(mem_ref, g_ref, wkv_ref, kt_ref, v_ref):
    mn = _rms(mem_ref[0], g_ref[...]).astype(BF16)
    kv = _dot(mn, wkv_ref[...])
    kt_ref[0] = kv[:, 0:D_MODEL].T.astype(BF16)
    v_ref[0] = kv[:, D_MODEL:2 * D_MODEL].astype(BF16)


def _kv(mem, layer, g, w_kv):
    bsz = mem.shape[0]
    return pl.pallas_call(
        _kv_kernel,
        grid=(bsz,),
        in_specs=[pl.BlockSpec((1, MEM_LEN, D_MODEL), lambda b: (b, 0, 0)),
                  _layer_spec(g.shape, layer), _layer_spec(w_kv.shape, layer)],
        out_specs=[pl.BlockSpec((1, D_MODEL, MEM_LEN), lambda b: (b, 0, 0)),
                   pl.BlockSpec((1, MEM_LEN, D_MODEL), lambda b: (b, 0, 0))],
        out_shape=[jax.ShapeDtypeStruct((bsz, D_MODEL, MEM_LEN), BF16),
                   jax.ShapeDtypeStruct((bsz, MEM_LEN, D_MODEL), BF16)],
        compiler_params=pltpu.CompilerParams(
            dimension_semantics=("parallel",), vmem_limit_bytes=VMEM_LIMIT_BYTES),
        name="memory_kv",
    )(mem, g, w_kv)


def _xattn_kernel(h_ref, g_ref, wq_ref, kt_ref, v_ref, wo_ref, out_ref, o_s):
    x = h_ref[0]
    u = _rms(x, g_ref[...]).astype(BF16)
    q = (_dot(u, wq_ref[...]) * (XATTN_HEAD_DIM ** -0.5)).astype(BF16)
    head_cols = [slice(hd * XATTN_HEAD_DIM, (hd + 1) * XATTN_HEAD_DIM) for hd in range(XATTN_HEADS)]
    scores = [_dot(q[:, cols], kt_ref[0, cols, :]) for cols in head_cols]
    for cols, s in zip(head_cols, scores):
        p = jnp.exp(s - jnp.max(s, axis=-1, keepdims=True))
        den = jnp.sum(p, axis=-1, keepdims=True)
        o = _dot(p.astype(BF16), v_ref[0, :, cols])
        o_s[:, cols] = (o / den).astype(BF16)
    out_ref[0] = x + _dot(o_s[...], wo_ref[...])


def _xattn(h, layer, g, w_q, kt, v, w_o):
    bsz, s, _ = h.shape
    tq = SEQ_BLOCK
    return pl.pallas_call(
        _xattn_kernel,
        grid=(bsz, s // tq),
        in_specs=[pl.BlockSpec((1, tq, D_MODEL), lambda b, j: (b, j, 0)),
                  _layer_spec(g.shape, layer), _layer_spec(w_q.shape, layer),
                  pl.BlockSpec((1, D_MODEL, MEM_LEN), lambda b, j: (b, 0, 0)),
                  pl.BlockSpec((1, MEM_LEN, D_MODEL), lambda b, j: (b, 0, 0)),
                  _layer_spec(w_o.shape, layer)],
        out_specs=pl.BlockSpec((1, tq, D_MODEL), lambda b, j: (b, j, 0)),
        out_shape=jax.ShapeDtypeStruct(h.shape, F32),
        scratch_shapes=[pltpu.VMEM((tq, D_MODEL), BF16)],
        compiler_params=pltpu.CompilerParams(
            dimension_semantics=("parallel", "parallel"), vmem_limit_bytes=VMEM_LIMIT_BYTES),
        name="xattn",
    )(h, g, w_q, kt, v, w_o)


def _ffn_kernel(h_ref, g_ref, wgu_ref, wd_ref, gf_ref, out_ref, a_s, *, final_norm):
    x = h_ref[...]
    u = _rms(x, g_ref[...]).astype(BF16)
    for ci in range(D_FF // FF_COL_CHUNK):
        lo = ci * FF_COL_CHUNK
        gate = _dot(u, wgu_ref[:, lo:lo + FF_COL_CHUNK])
        up = _dot(u, wgu_ref[:, D_FF + lo:D_FF + lo + FF_COL_CHUNK])
        a_s[:, lo:lo + FF_COL_CHUNK] = (_silu(gate) * up).astype(BF16)
    y = x + _dot(a_s[...], wd_ref[...])
    if final_norm:
        y = _rms(y, gf_ref[...])
    out_ref[...] = y


def _ffn(h2, layer, g, w_gate_up, w_down, g_final, final_norm):
    t = h2.shape[0]
    tm = FFN_BLOCK
    return pl.pallas_call(
        functools.partial(_ffn_kernel, final_norm=final_norm),
        grid=(t // tm,),
        in_specs=[pl.BlockSpec((tm, D_MODEL), lambda i: (i, 0)),
                  _layer_spec(g.shape, layer), _layer_spec(w_gate_up.shape, layer),
                  _layer_spec(w_down.shape, layer), _const_spec((1, D_MODEL))],
        out_specs=pl.BlockSpec((tm, D_MODEL), lambda i: (i, 0)),
        out_shape=jax.ShapeDtypeStruct(h2.shape, F32),
        scratch_shapes=[pltpu.VMEM((tm, D_FF), BF16)],
        compiler_params=pltpu.CompilerParams(
            dimension_semantics=("parallel",), vmem_limit_bytes=VMEM_LIMIT_BYTES),
        name="swiglu_final" if final_norm else "swiglu",
    )(h2, g, w_gate_up, w_down, g_final.reshape(1, -1))


def kernel(x, mem, mix_norm, w_in, conv_w, conv_b, dt_bias, a_log, d_skip, ssd_norm, pool_w, pool_scale,
           w_out_mix, xattn_norm, mem_norm, w_q, w_kv, w_o, ffn_norm, w_gate_up, w_down, final_norm):
    bsz, s, d = x.shape
    depth = w_in.shape[0]
    row = lambda a: a.reshape(depth, 1, -1)
    mixer_params = _prep_mixer_params(mix_norm, w_in, conv_w, conv_b, dt_bias, a_log, d_skip, ssd_norm, pool_w,
                                      pool_scale, w_out_mix)
    xattn_g, mem_g, ffn_g = row(xattn_norm), row(mem_norm), row(ffn_norm)
    wq_b, wkv_b, wo_b = w_q.astype(BF16), w_kv.astype(BF16), w_o.astype(BF16)
    wgu_b, wd_b = w_gate_up.astype(BF16), w_down.astype(BF16)
    h = x
    for l in range(depth):
        h = _mixer(h, l, mixer_params)
        kt, v = _kv(mem, l, mem_g, wkv_b)
        h = _xattn(h, l, xattn_g, wq_b, kt, v, wo_b)
        h = _ffn(h.reshape(bsz * s, d), l, ffn_g, wgu_b, wd_b, final_norm,
                 final_norm=(l == depth - 1)).reshape(bsz, s, d)
    return h
```

```python
import functools

import numpy as np
import jax
import jax.numpy as jnp
from jax import lax
from jax.experimental import pallas as pl
from jax.experimental.pallas import tpu as pltpu

F32 = jnp.float32
BF16 = jnp.bfloat16

D_MODEL = 1024
SSD_INNER = 1024
SSD_HEADS = 16
SSD_HEAD_DIM = 64
SSD_GROUPS = 2
SSD_STATE = 128
GROUP_WIDTH = SSD_INNER // SSD_GROUPS
CONV_WIDTH = 4
CONV_DIM = SSD_INNER + 2 * SSD_GROUPS * SSD_STATE
CHUNK = 128
POOL_WINDOWS = (2, 4, 8, 16)
POOL_WIDTH = 1024
POOL_GROUP_DIM = 256
D_MIX = 2048
DT_LANES = 128
DT_COPIES = 3
IN_COLS = SSD_INNER + CONV_DIM + POOL_WIDTH + DT_LANES
MEM_LEN = 256
XATTN_HEADS = 4
XATTN_HEAD_DIM = 256
D_FF = 2816
FF_COL_CHUNK = 1408
EPS = 1e-6
LOG2_E = 1.4426950408889634

ROW_MIX_NORM, ROW_CONV_W, ROW_CONV_B, ROW_DT_BIAS, ROW_A_LOG, ROW_D_SKIP, ROW_SSD_NORM, ROW_POOL_SCALE = (
    0, 1, 5, 6, 7, 8, 9, 10)
ROW_PARAM_ROWS = 16
CST_TRI = 0
CST_E64 = CST_TRI + DT_COPIES * CHUNK
CST_E128 = CST_E64 + SSD_INNER
CST_BAND = CST_E128 + SSD_HEADS * CHUNK

CONV_HALO = 8
CONV_COLS = 256
SEQ_BLOCK = 512
FFN_BLOCK = 512
VMEM_LIMIT_BYTES = 56 * 1024 * 1024


def _rms(x, g):
    ms = jnp.mean(x * x, axis=-1, keepdims=True)
    return x * lax.rsqrt(ms + EPS) * g


def _silu(x):
    return x / (1.0 + jnp.exp(-x))


def _softplus(x):
    return jnp.maximum(x, 0.0) + jnp.log1p(jnp.exp(-jnp.abs(x)))


def _dot(a, b):
    return jnp.dot(a, b, preferred_element_type=F32)


def _split3_f32(x):
    hi = x.astype(BF16).astype(F32)
    r1 = x - hi
    mid = r1.astype(BF16).astype(F32)
    return hi, mid, r1 - mid


def _pack3(x, lane):
    hi, mid, lo = _split3_f32(x)
    return jnp.where(lane < SSD_HEADS, hi, jnp.where(lane < 2 * SSD_HEADS, mid, lo)).astype(BF16)


def _mixer_kernel(h_ref, rowp_ref, win_ref, wout_ref, poolw_ref, cst_ref,
                  out_ref,
                  z_s, halo_s, xbcs_s, vf_s, vb_s, state_s, acst_s, pk_s, eacs_s, xdec_s, xlo_s, xhi_s,
                  bmt_s, cmb_s, cb_s, y_s, yoff_s, ycat_s, acc_s):
    ts = z_s.shape[0]
    nch = ts // CHUNK
    j = pl.program_id(1)

    @pl.when(j == 0)
    def _():
        state_s[...] = jnp.zeros_like(state_s)
        halo_s[...] = jnp.zeros_like(halo_s)
        vb_s[0:CHUNK, :] = jnp.zeros((CHUNK, POOL_WIDTH), BF16)

    def rows(c):
        return slice(c * CHUNK, (c + 1) * CHUNK)

    g_ref = rowp_ref.at[ROW_MIX_NORM:ROW_MIX_NORM + 1, 0:D_MODEL]
    convw_ref = rowp_ref.at[ROW_CONV_W:ROW_CONV_W + CONV_WIDTH, :]
    convb_ref = rowp_ref.at[ROW_CONV_B:ROW_CONV_B + 1, :]
    dtb_ref = rowp_ref.at[ROW_DT_BIAS:ROW_DT_BIAS + 1, 0:DT_LANES]
    alog_ref = rowp_ref.at[ROW_A_LOG:ROW_A_LOG + 1, 0:DT_LANES]
    dskip_ref = rowp_ref.at[ROW_D_SKIP:ROW_D_SKIP + 1, 0:SSD_INNER]
    ssdn_ref = rowp_ref.at[ROW_SSD_NORM:ROW_SSD_NORM + 1, 0:SSD_INNER]
    pscale_ref = rowp_ref.at[ROW_POOL_SCALE:ROW_POOL_SCALE + 1, 0:POOL_WIDTH]
    tri_ref = cst_ref.at[:, CST_TRI:CST_TRI + DT_COPIES * CHUNK]
    e64_ref = cst_ref.at[:, CST_E64:CST_E64 + SSD_INNER]
    e128_ref = cst_ref.at[:, CST_E128:CST_E128 + SSD_HEADS * CHUNK]

    def band(gi):
        return cst_ref[:, CST_BAND + gi * 2 * CHUNK:CST_BAND + (gi + 1) * 2 * CHUNK]

    x = h_ref[0]
    u = _rms(x, g_ref[...]).astype(BF16)

    v_off = SSD_INNER + CONV_DIM

    def z_piece(i):
        z_s[:, i * CONV_COLS:(i + 1) * CONV_COLS] = _dot(u, win_ref[:, i * CONV_COLS:(i + 1) * CONV_COLS])

    def v_piece(i):
        v = _dot(u, win_ref[:, v_off + i * CONV_COLS:v_off + (i + 1) * CONV_COLS])
        vf_s[:, i * CONV_COLS:(i + 1) * CONV_COLS] = v
        vb_s[CHUNK:CHUNK + ts, i * CONV_COLS:(i + 1) * CONV_COLS] = v.astype(BF16)

    def conv_group(cg):
        cols = slice(cg * CONV_COLS, (cg + 1) * CONV_COLS)
        res = _dot(u, win_ref[:, SSD_INNER + cg * CONV_COLS:SSD_INNER + (cg + 1) * CONV_COLS])
        w = [convw_ref[k:k + 1, cols] for k in range(CONV_WIDTH)]
        ext = jnp.concatenate([halo_s[:, cols], res], axis=0)
        prev = pltpu.roll(ext, 1, axis=0)
        b = w[1] * ext + w[0] * prev
        a = convb_ref[:, cols] + w[3] * res + w[2] * prev[CONV_HALO:, :]
        xbcs_s[:, cols] = _silu(a + pltpu.roll(b, 2, axis=0)[CONV_HALO:, :])
        halo_s[:, cols] = res[ts - CONV_HALO:ts, :]

    lane = lax.broadcasted_iota(jnp.int32, (CHUNK, CHUNK), 1)
    row = lax.broadcasted_iota(jnp.int32, (CHUNK, CHUNK), 0)
    causal = row >= lane
    lane_b = lax.broadcasted_iota(jnp.int32, (ts, DT_LANES), 1)
    a_row = jnp.where(lane[0:1, :] < DT_COPIES * SSD_HEADS, -jnp.exp(alog_ref[...]), 0.0)

    tpos0 = j * ts + 1 + lax.broadcasted_iota(jnp.int32, (CHUNK, POOL_GROUP_DIM), 0)

    def pool_group(gi):
        w = POOL_WINDOWS[gi]
        cols = slice(gi * POOL_GROUP_DIM, (gi + 1) * POOL_GROUP_DIM)
        ds = []
        for c in range(nch):
            win = _dot(band(gi), vb_s[c * CHUNK:(c + 2) * CHUNK, cols])
            cnt = jnp.minimum((tpos0 + c * CHUNK).astype(F32), float(w))
            ds.append((win / cnt - vf_s[rows(c), cols]).astype(BF16))
        yp = _dot(jnp.concatenate(ds, axis=0), poolw_ref[gi]) * pscale_ref[:, cols]
        ycat_s[:, SSD_INNER + gi * POOL_GROUP_DIM:SSD_INNER + (gi + 1) * POOL_GROUP_DIM] = yp.astype(BF16)

    def pool_out_piece(k):
        cols = slice(k * CONV_COLS, (k + 1) * CONV_COLS)
        acc_s[:, cols] = x[:, cols] + _dot(ycat_s[:, SSD_INNER:D_MIX], wout_ref[SSD_INNER:D_MIX, cols])

    n_side = SSD_INNER // CONV_COLS
    fillers = ([functools.partial(v_piece, k) for k in range(n_side)]
               + [functools.partial(pool_group, gi) for gi in range(len(POOL_WINDOWS))]
               + [functools.partial(pool_out_piece, k) for k in range(n_side)]
               + [functools.partial(z_piece, k) for k in range(n_side)])

    def fill(n):
        for _ in range(n):
            if fillers:
                fillers.pop(0)()

    dt = _softplus(_dot(u, win_ref[:, IN_COLS - DT_LANES:IN_COLS]) + dtb_ref[...])
    adt_parts = _split3_f32(dt * a_row)
    fill(2)
    conv_group(0)
    acs_c = []
    for c in range(nch):
        stacked = jnp.concatenate([p[rows(c)] for p in adt_parts], axis=0).astype(BF16)
        acs_c.append(_dot(tri_ref[...], stacked) * LOG2_E)
    fill(2)
    conv_group(1)
    for c in range(nch):
        acst_s[c] = acs_c[c].T
    pk_acs = _pack3(jnp.concatenate(acs_c, axis=0), lane_b)
    pk_s[...] = pk_acs
    pk_dt = _pack3(dt, lane_b)
    vb_tail = vb_s[ts:ts + CHUNK, :]
    fill(1)
    conv_group(2)
    acs_x = _dot(pk_acs, e64_ref[...])
    dt_x = _dot(pk_dt, e64_ref[...])
    fill(1)
    conv_group(3)
    eacs_s[...] = jnp.exp2(acs_x)
    fill(2)
    lane_x = lax.broadcasted_iota(jnp.int32, (CHUNK, SSD_INNER), 1) % (2 * SSD_HEAD_DIM)
    for c in range(nch):
        r = rows(c)
        xc = xbcs_s[r, 0:SSD_INNER] * dt_x[r]
        alast = acs_x[(c + 1) * CHUNK - 1:(c + 1) * CHUNK, :]
        xdec_s[r, :] = (xc * jnp.exp2(alast - acs_x[r])).astype(BF16)
        xlo_s[r, :] = jnp.where(lane_x < SSD_HEAD_DIM, xc, 0.0).astype(BF16)
        xhi_s[r, :] = jnp.where(lane_x >= SSD_HEAD_DIM, xc, 0.0).astype(BF16)
        fill(1)
        if c == 1:
            conv_group(4)
    conv_group(5)
    fill(1)
    for c in range(nch):
        for g in range(SSD_GROUPS):
            b_cols = slice(SSD_INNER + g * SSD_STATE, SSD_INNER + (g + 1) * SSD_STATE)
            c_cols = slice(SSD_INNER + (SSD_GROUPS + g) * SSD_STATE, SSD_INNER + (SSD_GROUPS + g + 1) * SSD_STATE)
            bmt_s[c, g] = xbcs_s[rows(c), b_cols].T.astype(BF16)
            cmb_s[c, g] = xbcs_s[rows(c), c_cols].astype(BF16)

    pairs_per_group = SSD_HEADS // SSD_GROUPS // 2
    for p in range(SSD_HEADS // 2):
        g = p // pairs_per_group
        pcols = slice(p * 2 * SSD_HEAD_DIM, (p + 1) * 2 * SSD_HEAD_DIM)
        if p % pairs_per_group == 0:
            for c in range(nch):
                cb_s[c] = _dot(cmb_s[c, g], bmt_s[c, g])
        cpair = _dot(pk_s[...], e128_ref[:, p * 2 * CHUNK:(p + 1) * 2 * CHUNK])
        for c in range(nch):
            ms = []
            for q in range(2):
                hh = 2 * p + q
                dmat = cpair[rows(c), q * CHUNK:(q + 1) * CHUNK] - acst_s[c, hh:hh + 1, :]
                seg = jnp.exp2(jnp.where(causal, dmat, -jnp.inf))
                ms.append((cb_s[c] * seg).astype(BF16))
            rhs = jnp.concatenate([xlo_s[rows(c), pcols], xhi_s[rows(c), pcols]], axis=0)
            y_s[rows(c), pcols] = _dot(jnp.concatenate(ms, axis=1), rhs)
        fill(1)
    fill(len(fillers))
    vb_s[0:CHUNK, :] = vb_tail

    new_st = {}
    for c in range(nch):
        for g in range(SSD_GROUPS):
            gcols = slice(g * GROUP_WIDTH, (g + 1) * GROUP_WIDTH)
            new_st[c, g] = _dot(bmt_s[c, g], xdec_s[rows(c), gcols])
    for c in range(nch):
        for g in range(SSD_GROUPS):
            gcols = slice(g * GROUP_WIDTH, (g + 1) * GROUP_WIDTH)
            st = state_s[g]
            yoff_s[rows(c), gcols] = _dot(cmb_s[c, g], st.astype(BF16))
            state_s[g] = st * eacs_s[(c + 1) * CHUNK - 1:(c + 1) * CHUNK, gcols] + new_st[c, g]

    for c in range(nch):
        r = rows(c)
        y = y_s[r, :] + yoff_s[r, :] * eacs_s[r, :] + dskip_ref[...] * xbcs_s[r, 0:SSD_INNER]
        y = y * _silu(z_s[r, :])
        ycat_s[r, 0:SSD_INNER] = _rms(y, ssdn_ref[...]).astype(BF16)
        out_ref[0, r, :] = acc_s[r, :] + _dot(ycat_s[r, 0:SSD_INNER], wout_ref[0:SSD_INNER, :])


def _const_spec(shape):
    nd = len(shape)
    return pl.BlockSpec(shape, lambda *_: (0,) * nd, pipeline_mode=pl.Buffered(1))


def _layer_spec(shape, layer):
    nd = len(shape)
    return pl.BlockSpec((None,) + tuple(shape[1:]), lambda *_: (layer,) + (0,) * (nd - 1),
                        pipeline_mode=pl.Buffered(1))


def _mixer_constants():
    t = np.arange(CHUNK)
    tri = (t[:, None] >= t[None, :]).astype(np.float32)
    tri3 = np.concatenate([tri] * DT_COPIES, axis=1)
    e64 = np.zeros((DT_LANES, SSD_INNER), np.float32)
    e128 = np.zeros((DT_LANES, SSD_HEADS * CHUNK), np.float32)
    for c in range(DT_COPIES):
        for h in range(SSD_HEADS):
            e64[c * SSD_HEADS + h, h * SSD_HEAD_DIM:(h + 1) * SSD_HEAD_DIM] = 1.0
            e128[c * SSD_HEADS + h, h * CHUNK:(h + 1) * CHUNK] = 1.0
    k = np.arange(2 * CHUNK) - CHUNK
    band = [((k[None, :] <= t[:, None]) & (k[None, :] >= t[:, None] - w + 1)).astype(np.float32)
            for w in POOL_WINDOWS]
    return jnp.asarray(np.concatenate([tri3, e64, e128] + band, axis=1), BF16)


def _prep_mixer_params(mix_norm, w_in, conv_w, conv_b, dt_bias, a_log, d_skip, ssd_norm, pool_w, pool_scale,
                       w_out_mix):
    depth = w_in.shape[0]
    xbc_end = SSD_INNER + CONV_DIM
    dt_end = xbc_end + SSD_HEADS
    w_dt = w_in[:, :, xbc_end:dt_end]
    n_pad = DT_LANES - DT_COPIES * SSD_HEADS
    w_dt3 = jnp.concatenate([w_dt] * DT_COPIES + [jnp.zeros((depth, D_MODEL, n_pad), F32)], axis=2)
    win = jnp.concatenate([w_in[:, :, :xbc_end], w_in[:, :, dt_end:], w_dt3], axis=2).astype(BF16)
    pad = jnp.zeros((depth, n_pad), F32)

    def row(a):
        return jnp.pad(a, ((0, 0), (0, CONV_DIM - a.shape[1])))[:, None, :]

    rowp = jnp.concatenate([
        row(mix_norm), conv_w, row(conv_b),
        row(jnp.concatenate([dt_bias] * DT_COPIES + [pad], axis=1)),
        row(jnp.concatenate([a_log] * DT_COPIES + [pad], axis=1)),
        row(jnp.repeat(d_skip, SSD_HEAD_DIM, axis=1)), row(ssd_norm), row(pool_scale),
        jnp.zeros((depth, ROW_PARAM_ROWS - ROW_POOL_SCALE - 1, CONV_DIM), F32)], axis=1)
    return rowp, win, w_out_mix.astype(BF16), pool_w.astype(BF16)


def _mixer(h, layer, params):
    bsz, s, _ = h.shape
    ts = SEQ_BLOCK
    nch = ts // CHUNK
    consts = _mixer_constants()
    in_specs = [pl.BlockSpec((1, ts, D_MODEL), lambda b, j: (b, j, 0))]
    in_specs += [_layer_spec(a.shape, layer) for a in params]
    in_specs += [_const_spec(consts.shape)]
    return pl.pallas_call(
        _mixer_kernel,
        grid=(bsz, s // ts),
        in_specs=in_specs,
        out_specs=pl.BlockSpec((1, ts, D_MODEL), lambda b, j: (b, j, 0)),
        out_shape=jax.ShapeDtypeStruct(h.shape, F32),
        scratch_shapes=[
            pltpu.VMEM((ts, SSD_INNER), F32),
            pltpu.VMEM((CONV_HALO, CONV_DIM), F32),
            pltpu.VMEM((ts, CONV_DIM), F32),
            pltpu.VMEM((ts, POOL_WIDTH), F32),
            pltpu.VMEM((CHUNK + ts, POOL_WIDTH), BF16),
            pltpu.VMEM((SSD_GROUPS, SSD_STATE, GROUP_WIDTH), F32),
            pltpu.VMEM((nch, CHUNK, CHUNK), F32),
            pltpu.VMEM((ts, DT_LANES), BF16),
            pltpu.VMEM((ts, SSD_INNER), F32),
            pltpu.VMEM((ts, SSD_INNER), BF16),
            pltpu.VMEM((ts, SSD_INNER), BF16),
            pltpu.VMEM((ts, SSD_INNER), BF16),
            pltpu.VMEM((nch, SSD_GROUPS, SSD_STATE, CHUNK), BF16),
            pltpu.VMEM((nch, SSD_GROUPS, CHUNK, SSD_STATE), BF16),
            pltpu.VMEM((nch, CHUNK, CHUNK), F32),
            pltpu.VMEM((ts, SSD_INNER), F32),
            pltpu.VMEM((ts, SSD_INNER), F32),
            pltpu.VMEM((ts, D_MIX), BF16),
            pltpu.VMEM((ts, D_MODEL), F32),
        ],
        compiler_params=pltpu.CompilerParams(
            dimension_semantics=("parallel", "arbitrary"), vmem_limit_bytes=VMEM_LIMIT_BYTES),
        name="mixer",
    )(h, *params, consts)


def _kv_kernel(mem_ref, g_ref, wkv_ref, kt_ref, v_ref):
    mn = _rms(mem_ref[0], g_ref[...]).astype(BF16)
    kv = _dot(mn, wkv_ref[...])
    kt_ref[0] = kv[:, 0:D_MODEL].T.astype(BF16)
    v_ref[0] = kv[:, D_MODEL:2 * D_MODEL].astype(BF16)


def _kv(mem, layer, g, w_kv):
    bsz = mem.shape[0]
    return pl.pallas_call(
        _kv_kernel,
        grid=(bsz,),
        in_specs=[pl.BlockSpec((1, MEM_LEN, D_MODEL), lambda b: (b, 0, 0)),
                  _layer_spec(g.shape, layer), _layer_spec(w_kv.shape, layer)],
        out_specs=[pl.BlockSpec((1, D_MODEL, MEM_LEN), lambda b: (b, 0, 0)),
                   pl.BlockSpec((1, MEM_LEN, D_MODEL), lambda b: (b, 0, 0))],
        out_shape=[jax.ShapeDtypeStruct((bsz, D_MODEL, MEM_LEN), BF16),
                   jax.ShapeDtypeStruct((bsz, MEM_LEN, D_MODEL), BF16)],
        compiler_params=pltpu.CompilerParams(
            dimension_semantics=("parallel",), vmem_limit_bytes=VMEM_LIMIT_BYTES),
        name="memory_kv",
    )(mem, g, w_kv)


def _xattn_kernel(h_ref, g_ref, wq_ref, kt_ref, v_ref, wo_ref, out_ref, o_s):
    x = h_ref[0]
    u = _rms(x, g_ref[...]).astype(BF16)
    q = (_dot(u, wq_ref[...]) * (XATTN_HEAD_DIM ** -0.5)).astype(BF16)
    head_cols = [slice(hd * XATTN_HEAD_DIM, (hd + 1) * XATTN_HEAD_DIM) for hd in range(XATTN_HEADS)]
    scores = [_dot(q[:, cols], kt_ref[0, cols, :]) for cols in head_cols]
    for cols, s in zip(head_cols, scores):
        p = jnp.exp(s - jnp.max(s, axis=-1, keepdims=True))
        den = jnp.sum(p, axis=-1, keepdims=True)
        o = _dot(p.astype(BF16), v_ref[0, :, cols])
        o_s[:, cols] = (o / den).astype(BF16)
    out_ref[0] = x + _dot(o_s[...], wo_ref[...])


def _xattn(h, layer, g, w_q, kt, v, w_o):
    bsz, s, _ = h.shape
    tq = SEQ_BLOCK
    return pl.pallas_call(
        _xattn_kernel,
        grid=(bsz, s // tq),
        in_specs=[pl.BlockSpec((1, tq, D_MODEL), lambda b, j: (b, j, 0)),
                  _layer_spec(g.shape, layer), _layer_spec(w_q.shape, layer),
                  pl.BlockSpec((1, D_MODEL, MEM_LEN), lambda b, j: (b, 0, 0)),
                  pl.BlockSpec((1, MEM_LEN, D_MODEL), lambda b, j: (b, 0, 0)),
                  _layer_spec(w_o.shape, layer)],
        out_specs=pl.BlockSpec((1, tq, D_MODEL), lambda b, j: (b, j, 0)),
        out_shape=jax.ShapeDtypeStruct(h.shape, F32),
        scratch_shapes=[pltpu.VMEM((tq, D_MODEL), BF16)],
        compiler_params=pltpu.CompilerParams(
            dimension_semantics=("parallel", "parallel"), vmem_limit_bytes=VMEM_LIMIT_BYTES),
        name="xattn",
    )(h, g, w_q, kt, v, w_o)


def _ffn_kernel(h_ref, g_ref, wgu_ref, wd_ref, gf_ref, out_ref, a_s, *, final_norm):
    x = h_ref[...]
    u = _rms(x, g_ref[...]).astype(BF16)
    for ci in range(D_FF // FF_COL_CHUNK):
        lo = ci * FF_COL_CHUNK
        gate = _dot(u, wgu_ref[:, lo:lo + FF_COL_CHUNK])
        up = _dot(u, wgu_ref[:, D_FF + lo:D_FF + lo + FF_COL_CHUNK])
        a_s[:, lo:lo + FF_COL_CHUNK] = (_silu(gate) * up).astype(BF16)
    y = x + _dot(a_s[...], wd_ref[...])
    if final_norm:
        y = _rms(y, gf_ref[...])
    out_ref[...] = y


def _ffn(h2, layer, g, w_gate_up, w_down, g_final, final_norm):
    t = h2.shape[0]
    tm = FFN_BLOCK
    return pl.pallas_call(
        functools.partial(_ffn_kernel, final_norm=final_norm),
        grid=(t // tm,),
        in_specs=[pl.BlockSpec((tm, D_MODEL), lambda i: (i, 0)),
                  _layer_spec(g.shape, layer), _layer_spec(w_gate_up.shape, layer),
                  _layer_spec(w_down.shape, layer), _const_spec((1, D_MODEL))],
        out_specs=pl.BlockSpec((tm, D_MODEL), lambda i: (i, 0)),
        out_shape=jax.ShapeDtypeStruct(h2.shape, F32),
        scratch_shapes=[pltpu.VMEM((tm, D_FF), BF16)],
        compiler_params=pltpu.CompilerParams(
            dimension_semantics=("parallel",), vmem_limit_bytes=VMEM_LIMIT_BYTES),
        name="swiglu_final" if final_norm else "swiglu",
    )(h2, g, w_gate_up, w_down, g_final.reshape(1, -1))


def kernel(x, mem, mix_norm, w_in, conv_w, conv_b, dt_bias, a_log, d_skip, ssd_norm, pool_w, pool_scale,
           w_out_mix, xattn_norm, mem_norm, w_q, w_kv, w_o, ffn_norm, w_gate_up, w_down, final_norm):
    bsz, s, d = x.shape
    depth = w_in.shape[0]
    row = lambda a: a.reshape(depth, 1, -1)
    mixer_params = _prep_mixer_params(mix_norm, w_in, conv_w, conv_b, dt_bias, a_log, d_skip, ssd_norm, pool_w,
                                      pool_scale, w_out_mix)
    xattn_g, mem_g, ffn_g = row(xattn_norm), row(mem_norm), row(ffn_norm)
    wq_b, wkv_b, wo_b = w_q.astype(BF16), w_kv.astype(BF16), w_o.astype(BF16)
    wgu_b, wd_b = w_gate_up.astype(BF16), w_down.astype(BF16)
    h = x
    for l in range(depth):
        h = _mixer(h, l, mixer_params)
        kt, v = _kv(mem, l, mem_g, wkv_b)
        h = _xattn(h, l, xattn_g, wq_b, kt, v, wo_b)
        h = _ffn(h.reshape(bsz * s, d), l, ffn_g, wgu_b, wd_b, final_norm,
                 final_norm=(l == depth - 1)).reshape(bsz, s, d)
    return h
```

```python
import functools

import numpy as np
import jax
import jax.numpy as jnp
from jax import lax
from jax.experimental import pallas as pl
from jax.experimental.pallas import tpu as pltpu

F32 = jnp.float32
BF16 = jnp.bfloat16

D_MODEL = 1024
SSD_INNER = 1024
SSD_HEADS = 16
SSD_HEAD_DIM = 64
SSD_GROUPS = 2
SSD_STATE = 128
GROUP_WIDTH = SSD_INNER // SSD_GROUPS
CONV_WIDTH = 4
CONV_DIM = SSD_INNER + 2 * SSD_GROUPS * SSD_STATE
CHUNK = 128
POOL_WINDOWS = (2, 4, 8, 16)
POOL_WIDTH = 1024
POOL_GROUP_DIM = 256
D_MIX = 2048
DT_LANES = 128
DT_COPIES = 3
IN_COLS = SSD_INNER + CONV_DIM + POOL_WIDTH + DT_LANES
MEM_LEN = 256
XATTN_HEADS = 4
XATTN_HEAD_DIM = 256
D_FF = 2816
FF_COL_CHUNK = 1408
EPS = 1e-6
LOG2_E = 1.4426950408889634

ROW_MIX_NORM, ROW_CONV_W, ROW_CONV_B, ROW_DT_BIAS, ROW_A_LOG, ROW_D_SKIP, ROW_SSD_NORM, ROW_POOL_SCALE = (
    0, 1, 5, 6, 7, 8, 9, 10)
ROW_PARAM_ROWS = 16
CST_TRI = 0
CST_E64 = CST_TRI + DT_COPIES * CHUNK
CST_E128 = CST_E64 + SSD_INNER
CST_BAND = CST_E128 + SSD_HEADS * CHUNK

CONV_HALO = 8
CONV_COLS = 256
SEQ_BLOCK = 512
FFN_BLOCK = 512
VMEM_LIMIT_BYTES = 56 * 1024 * 1024


def _rms(x, g):
    ms = jnp.mean(x * x, axis=-1, keepdims=True)
    return x * lax.rsqrt(ms + EPS) * g


def _silu(x):
    h = 0.5 * x
    return h + h * jnp.tanh(h)


def _softplus(x):
    return jnp.maximum(x, 0.0) + jnp.log1p(jnp.exp(-jnp.abs(x)))


def _dot(a, b):
    return jnp.dot(a, b, preferred_element_type=F32)


def _split3_f32(x):
    hi = x.astype(BF16).astype(F32)
    r1 = x - hi
    mid = r1.astype(BF16).astype(F32)
    return hi, mid, r1 - mid


def _pack3(x, lane):
    hi, mid, lo = _split3_f32(x)
    return jnp.where(lane < SSD_HEADS, hi, jnp.where(lane < 2 * SSD_HEADS, mid, lo)).astype(BF16)


def _mixer_kernel(h_ref, rowp_ref, win_ref, wout_ref, poolw_ref, cst_ref,
                  out_ref,
                  z_s, halo_s, xbcs_s, vf_s, vb_s, state_s, acst_s, pk_s, eacs_s, xdec_s, xlo_s, xhi_s,
                  bmt_s, cmb_s, cb_s, y_s, yoff_s, ycat_s, acc_s):
    ts = z_s.shape[0]
    nch = ts // CHUNK
    j = pl.program_id(1)

    @pl.when(j == 0)
    def _():
        state_s[...] = jnp.zeros_like(state_s)
        halo_s[...] = jnp.zeros_like(halo_s)
        vb_s[0:CHUNK, :] = jnp.zeros((CHUNK, POOL_WIDTH), BF16)

    def rows(c):
        return slice(c * CHUNK, (c + 1) * CHUNK)

    g_ref = rowp_ref.at[ROW_MIX_NORM:ROW_MIX_NORM + 1, 0:D_MODEL]
    convw_ref = rowp_ref.at[ROW_CONV_W:ROW_CONV_W + CONV_WIDTH, :]
    convb_ref = rowp_ref.at[ROW_CONV_B:ROW_CONV_B + 1, :]
    dtb_ref = rowp_ref.at[ROW_DT_BIAS:ROW_DT_BIAS + 1, 0:DT_LANES]
    alog_ref = rowp_ref.at[ROW_A_LOG:ROW_A_LOG + 1, 0:DT_LANES]
    dskip_ref = rowp_ref.at[ROW_D_SKIP:ROW_D_SKIP + 1, 0:SSD_INNER]
    ssdn_ref = rowp_ref.at[ROW_SSD_NORM:ROW_SSD_NORM + 1, 0:SSD_INNER]
    pscale_ref = rowp_ref.at[ROW_POOL_SCALE:ROW_POOL_SCALE + 1, 0:POOL_WIDTH]
    tri_ref = cst_ref.at[:, CST_TRI:CST_TRI + DT_COPIES * CHUNK]
    e64_ref = cst_ref.at[:, CST_E64:CST_E64 + SSD_INNER]
    e128_ref = cst_ref.at[:, CST_E128:CST_E128 + SSD_HEADS * CHUNK]

    def band(gi):
        return cst_ref[:, CST_BAND + gi * 2 * CHUNK:CST_BAND + (gi + 1) * 2 * CHUNK]

    x = h_ref[0]
    u = _rms(x, g_ref[...]).astype(BF16)

    v_off = SSD_INNER + CONV_DIM

    def z_piece(i):
        z_s[:, i * CONV_COLS:(i + 1) * CONV_COLS] = _dot(u, win_ref[:, i * CONV_COLS:(i + 1) * CONV_COLS])

    def v_piece(i):
        v = _dot(u, win_ref[:, v_off + i * CONV_COLS:v_off + (i + 1) * CONV_COLS])
        vf_s[:, i * CONV_COLS:(i + 1) * CONV_COLS] = v
        vb_s[CHUNK:CHUNK + ts, i * CONV_COLS:(i + 1) * CONV_COLS] = v.astype(BF16)

    def conv_group(cg):
        cols = slice(cg * CONV_COLS, (cg + 1) * CONV_COLS)
        res = _dot(u, win_ref[:, SSD_INNER + cg * CONV_COLS:SSD_INNER + (cg + 1) * CONV_COLS])
        w = [convw_ref[k:k + 1, cols] for k in range(CONV_WIDTH)]
        ext = jnp.concatenate([halo_s[:, cols], res], axis=0)
        prev = pltpu.roll(ext, 1, axis=0)
        b = w[1] * ext + w[0] * prev
        a = convb_ref[:, cols] + w[3] * res + w[2] * prev[CONV_HALO:, :]
        xbcs_s[:, cols] = _silu(a + pltpu.roll(b, 2, axis=0)[CONV_HALO:, :])
        halo_s[:, cols] = res[ts - CONV_HALO:ts, :]

    lane = lax.broadcasted_iota(jnp.int32, (CHUNK, CHUNK), 1)
    row = lax.broadcasted_iota(jnp.int32, (CHUNK, CHUNK), 0)
    causal = row >= lane
    lane_b = lax.broadcasted_iota(jnp.int32, (ts, DT_LANES), 1)
    a_row = jnp.where(lane[0:1, :] < DT_COPIES * SSD_HEADS, -jnp.exp(alog_ref[...]), 0.0)

    tpos0 = j * ts + 1 + lax.broadcasted_iota(jnp.int32, (CHUNK, POOL_GROUP_DIM), 0)

    def pool_group(gi):
        w = POOL_WINDOWS[gi]
        cols = slice(gi * POOL_GROUP_DIM, (gi + 1) * POOL_GROUP_DIM)
        ds = []
        for c in range(nch):
            win = _dot(band(gi), vb_s[c * CHUNK:(c + 2) * CHUNK, cols])
            if c == 0:
                mean = win / jnp.minimum(tpos0.astype(F32), float(w))
            else:
                mean = win * (1.0 / w)
            ds.append((mean - vf_s[rows(c), cols]).astype(BF16))
        yp = _dot(jnp.concatenate(ds, axis=0), poolw_ref[gi]) * pscale_ref[:, cols]
        ycat_s[:, SSD_INNER + gi * POOL_GROUP_DIM:SSD_INNER + (gi + 1) * POOL_GROUP_DIM] = yp.astype(BF16)

    def pool_out_piece(k):
        cols = slice(k * CONV_COLS, (k + 1) * CONV_COLS)
        acc_s[:, cols] = x[:, cols] + _dot(ycat_s[:, SSD_INNER:D_MIX], wout_ref[SSD_INNER:D_MIX, cols])

    n_side = SSD_INNER // CONV_COLS
    fillers = ([functools.partial(v_piece, k) for k in range(n_side)]
               + [functools.partial(pool_group, gi) for gi in range(len(POOL_WINDOWS))]
               + [functools.partial(pool_out_piece, k) for k in range(n_side)]
               + [functools.partial(z_piece, k) for k in range(n_side)])

    def fill(n):
        for _ in range(n):
            if fillers:
                fillers.pop(0)()

    dt = _softplus(_dot(u, win_ref[:, IN_COLS - DT_LANES:IN_COLS]) + dtb_ref[...])
    adt_parts = _split3_f32(dt * a_row)
    fill(2)
    conv_group(0)
    acs_c = []
    for c in range(nch):
        stacked = jnp.concatenate([p[rows(c)] for p in adt_parts], axis=0).astype(BF16)
        acs_c.append(_dot(tri_ref[...], stacked) * LOG2_E)
    fill(2)
    conv_group(1)
    for c in range(nch):
        acst_s[c] = acs_c[c].T
    pk_acs = _pack3(jnp.concatenate(acs_c, axis=0), lane_b)
    pk_s[...] = pk_acs
    pk_dt = _pack3(dt, lane_b)
    vb_tail = vb_s[ts:ts + CHUNK, :]
    fill(1)
    conv_group(2)
    acs_x = _dot(pk_acs, e64_ref[...])
    dt_x = _dot(pk_dt, e64_ref[...])
    fill(1)
    conv_group(3)
    eacs_s[...] = jnp.exp2(acs_x)
    fill(2)
    lane_x = lax.broadcasted_iota(jnp.int32, (CHUNK, SSD_INNER), 1) % (2 * SSD_HEAD_DIM)
    for c in range(nch):
        r = rows(c)
        xc = xbcs_s[r, 0:SSD_INNER] * dt_x[r]
        alast = acs_x[(c + 1) * CHUNK - 1:(c + 1) * CHUNK, :]
        xdec_s[r, :] = (xc * jnp.exp2(alast - acs_x[r])).astype(BF16)
        xlo_s[r, :] = jnp.where(lane_x < SSD_HEAD_DIM, xc, 0.0).astype(BF16)
        xhi_s[r, :] = jnp.where(lane_x >= SSD_HEAD_DIM, xc, 0.0).astype(BF16)
        fill(1)
        if c == 1:
            conv_group(4)
    conv_group(5)
    fill(1)
    for c in range(nch):
        for g in range(SSD_GROUPS):
            b_cols = slice(SSD_INNER + g * SSD_STATE, SSD_INNER + (g + 1) * SSD_STATE)
            c_cols = slice(SSD_INNER + (SSD_GROUPS + g) * SSD_STATE, SSD_INNER + (SSD_GROUPS + g + 1) * SSD_STATE)
            bmt_s[c, g] = xbcs_s[rows(c), b_cols].T.astype(BF16)
            cmb_s[c, g] = xbcs_s[rows(c), c_cols].astype(BF16)

    pairs_per_group = SSD_HEADS // SSD_GROUPS // 2
    for p in range(SSD_HEADS // 2):
        g = p // pairs_per_group
        pcols = slice(p * 2 * SSD_HEAD_DIM, (p + 1) * 2 * SSD_HEAD_DIM)
        if p % pairs_per_group == 0:
            for c in range(nch):
                cb_s[c] = _dot(cmb_s[c, g], bmt_s[c, g])
        cpair = _dot(pk_s[...], e128_ref[:, p * 2 * CHUNK:(p + 1) * 2 * CHUNK])
        for c in range(nch):
            ms = []
            for q in range(2):
                hh = 2 * p + q
                dmat = cpair[rows(c), q * CHUNK:(q + 1) * CHUNK] - acst_s[c, hh:hh + 1, :]
                seg = jnp.exp2(jnp.where(causal, dmat, -jnp.inf))
                ms.append((cb_s[c] * seg).astype(BF16))
            rhs = jnp.concatenate([xlo_s[rows(c), pcols], xhi_s[rows(c), pcols]], axis=0)
            y_s[rows(c), pcols] = _dot(jnp.concatenate(ms, axis=1), rhs)
        if p < 1:
            fill(1)
    vb_s[0:CHUNK, :] = vb_tail

    new_st = {}
    for c in range(nch):
        for g in range(SSD_GROUPS):
            gcols = slice(g * GROUP_WIDTH, (g + 1) * GROUP_WIDTH)
            new_st[c, g] = _dot(bmt_s[c, g], xdec_s[rows(c), gcols])
    fill(len(fillers))
    for c in range(nch):
        for g in range(SSD_GROUPS):
            gcols = slice(g * GROUP_WIDTH, (g + 1) * GROUP_WIDTH)
            st = state_s[g]
            yoff_s[rows(c), gcols] = _dot(cmb_s[c, g], st.astype(BF16))
            state_s[g] = st * eacs_s[(c + 1) * CHUNK - 1:(c + 1) * CHUNK, gcols] + new_st[c, g]

    for c in range(nch):
        r = rows(c)
        y = y_s[r, :] + yoff_s[r, :] * eacs_s[r, :] + dskip_ref[...] * xbcs_s[r, 0:SSD_INNER]
        y = y * _silu(z_s[r, :])
        ycat_s[r, 0:SSD_INNER] = _rms(y, ssdn_ref[...]).astype(BF16)
        out_ref[0, r, :] = acc_s[r, :] + _dot(ycat_s[r, 0:SSD_INNER], wout_ref[0:SSD_INNER, :])


def _const_spec(shape):
    nd = len(shape)
    return pl.BlockSpec(shape, lambda *_: (0,) * nd, pipeline_mode=pl.Buffered(1))


def _layer_spec(shape, layer):
    nd = len(shape)
    return pl.BlockSpec((None,) + tuple(shape[1:]), lambda *_: (layer,) + (0,) * (nd - 1),
                        pipeline_mode=pl.Buffered(1))


def _mixer_constants():
    t = np.arange(CHUNK)
    tri = (t[:, None] >= t[None, :]).astype(np.float32)
    tri3 = np.concatenate([tri] * DT_COPIES, axis=1)
    e64 = np.zeros((DT_LANES, SSD_INNER), np.float32)
    e128 = np.zeros((DT_LANES, SSD_HEADS * CHUNK), np.float32)
    for c in range(DT_COPIES):
        for h in range(SSD_HEADS):
            e64[c * SSD_HEADS + h, h * SSD_HEAD_DIM:(h + 1) * SSD_HEAD_DIM] = 1.0
            e128[c * SSD_HEADS + h, h * CHUNK:(h + 1) * CHUNK] = 1.0
    k = np.arange(2 * CHUNK) - CHUNK
    band = [((k[None, :] <= t[:, None]) & (k[None, :] >= t[:, None] - w + 1)).astype(np.float32)
            for w in POOL_WINDOWS]
    return jnp.asarray(np.concatenate([tri3, e64, e128] + band, axis=1), BF16)


def _prep_mixer_params(mix_norm, w_in, conv_w, conv_b, dt_bias, a_log, d_skip, ssd_norm, pool_w, pool_scale,
                       w_out_mix):
    depth = w_in.shape[0]
    xbc_end = SSD_INNER + CONV_DIM
    dt_end = xbc_end + SSD_HEADS
    w_dt = w_in[:, :, xbc_end:dt_end]
    n_pad = DT_LANES - DT_COPIES * SSD_HEADS
    w_dt3 = jnp.concatenate([w_dt] * DT_COPIES + [jnp.zeros((depth, D_MODEL, n_pad), F32)], axis=2)
    win = jnp.concatenate([w_in[:, :, :xbc_end].astype(BF16), w_in[:, :, dt_end:].astype(BF16),
                           w_dt3.astype(BF16)], axis=2)
    pad = jnp.zeros((depth, n_pad), F32)

    def row(a):
        return jnp.pad(a, ((0, 0), (0, CONV_DIM - a.shape[1])))[:, None, :]

    rowp = jnp.concatenate([
        row(mix_norm), conv_w, row(conv_b),
        row(jnp.concatenate([dt_bias] * DT_COPIES + [pad], axis=1)),
        row(jnp.concatenate([a_log] * DT_COPIES + [pad], axis=1)),
        row(jnp.repeat(d_skip, SSD_HEAD_DIM, axis=1)), row(ssd_norm), row(pool_scale),
        jnp.zeros((depth, ROW_PARAM_ROWS - ROW_POOL_SCALE - 1, CONV_DIM), F32)], axis=1)
    return rowp, win, w_out_mix.astype(BF16), pool_w.astype(BF16)


def _mixer(h, layer, params):
    bsz, s, _ = h.shape
    ts = SEQ_BLOCK
    nch = ts // CHUNK
    consts = _mixer_constants()
    in_specs = [pl.BlockSpec((1, ts, D_MODEL), lambda b, j: (b, j, 0))]
    in_specs += [_layer_spec(a.shape, layer) for a in params]
    in_specs += [_const_spec(consts.shape)]
    return pl.pallas_call(
        _mixer_kernel,
        grid=(bsz, s // ts),
        in_specs=in_specs,
        out_specs=pl.BlockSpec((1, ts, D_MODEL), lambda b, j: (b, j, 0)),
        out_shape=jax.ShapeDtypeStruct(h.shape, F32),
        scratch_shapes=[
            pltpu.VMEM((ts, SSD_INNER), F32),
            pltpu.VMEM((CONV_HALO, CONV_DIM), F32),
            pltpu.VMEM((ts, CONV_DIM), F32),
            pltpu.VMEM((ts, POOL_WIDTH), F32),
            pltpu.VMEM((CHUNK + ts, POOL_WIDTH), BF16),
            pltpu.VMEM((SSD_GROUPS, SSD_STATE, GROUP_WIDTH), F32),
            pltpu.VMEM((nch, CHUNK, CHUNK), F32),
            pltpu.VMEM((ts, DT_LANES), BF16),
            pltpu.VMEM((ts, SSD_INNER), F32),
            pltpu.VMEM((ts, SSD_INNER), BF16),
            pltpu.VMEM((ts, SSD_INNER), BF16),
            pltpu.VMEM((ts, SSD_INNER), BF16),
            pltpu.VMEM((nch, SSD_GROUPS, SSD_STATE, CHUNK), BF16),
            pltpu.VMEM((nch, SSD_GROUPS, CHUNK, SSD_STATE), BF16),
            pltpu.VMEM((nch, CHUNK, CHUNK), F32),
            pltpu.VMEM((ts, SSD_INNER), F32),
            pltpu.VMEM((ts, SSD_INNER), F32),
            pltpu.VMEM((ts, D_MIX), BF16),
            pltpu.VMEM((ts, D_MODEL), F32),
        ],
        compiler_params=pltpu.CompilerParams(
            dimension_semantics=("parallel", "arbitrary"), vmem_limit_bytes=VMEM_LIMIT_BYTES),
        name="mixer",
    )(h, *params, consts)


def _kv_kernel(mem_ref, g_ref, wkv_ref, kt_ref, v_ref):
    mn = _rms(mem_ref[0], g_ref[...]).astype(BF16)
    kv = _dot(mn, wkv_ref[...])
    kt_ref[0] = kv[:, 0:D_MODEL].T.astype(BF16)
    v_ref[0] = kv[:, D_MODEL:2 * D_MODEL].astype(BF16)


def _kv(mem, layer, g, w_kv):
    bsz = mem.shape[0]
    return pl.pallas_call(
        _kv_kernel,
        grid=(bsz,),
        in_specs=[pl.BlockSpec((1, MEM_LEN, D_MODEL), lambda b: (b, 0, 0)),
                  _layer_spec(g.shape, layer), _layer_spec(w_kv.shape, layer)],
        out_specs=[pl.BlockSpec((1, D_MODEL, MEM_LEN), lambda b: (b, 0, 0)),
                   pl.BlockSpec((1, MEM_LEN, D_MODEL), lambda b: (b, 0, 0))],
        out_shape=[jax.ShapeDtypeStruct((bsz, D_MODEL, MEM_LEN), BF16),
                   jax.ShapeDtypeStruct((bsz, MEM_LEN, D_MODEL), BF16)],
        compiler_params=pltpu.CompilerParams(
            dimension_semantics=("parallel",), vmem_limit_bytes=VMEM_LIMIT_BYTES),
        name="memory_kv",
    )(mem, g, w_kv)


def _xattn_kernel(h_ref, g_ref, wq_ref, kt_ref, v_ref, wo_ref, out_ref, o_s):
    x = h_ref[0]
    u = _rms(x, g_ref[...]).astype(BF16)
    q = (_dot(u, wq_ref[...]) * (XATTN_HEAD_DIM ** -0.5)).astype(BF16)
    head_cols = [slice(hd * XATTN_HEAD_DIM, (hd + 1) * XATTN_HEAD_DIM) for hd in range(XATTN_HEADS)]
    scores = [_dot(q[:, cols], kt_ref[0, cols, :]) for cols in head_cols]
    for cols, s in zip(head_cols, scores):
        p = jnp.exp(s - jnp.max(s, axis=-1, keepdims=True))
        den = jnp.sum(p, axis=-1, keepdims=True)
        o = _dot(p.astype(BF16), v_ref[0, :, cols])
        o_s[:, cols] = (o * (1.0 / den)).astype(BF16)
    out_ref[0] = x + _dot(o_s[...], wo_ref[...])


def _xattn(h, layer, g, w_q, kt, v, w_o):
    bsz, s, _ = h.shape
    tq = SEQ_BLOCK
    return pl.pallas_call(
        _xattn_kernel,
        grid=(bsz, s // tq),
        in_specs=[pl.BlockSpec((1, tq, D_MODEL), lambda b, j: (b, j, 0)),
                  _layer_spec(g.shape, layer), _layer_spec(w_q.shape, layer),
                  pl.BlockSpec((1, D_MODEL, MEM_LEN), lambda b, j: (b, 0, 0)),
                  pl.BlockSpec((1, MEM_LEN, D_MODEL), lambda b, j: (b, 0, 0)),
                  _layer_spec(w_o.shape, layer)],
        out_specs=pl.BlockSpec((1, tq, D_MODEL), lambda b, j: (b, j, 0)),
        out_shape=jax.ShapeDtypeStruct(h.shape, F32),
        scratch_shapes=[pltpu.VMEM((tq, D_MODEL), BF16)],
        compiler_params=pltpu.CompilerParams(
            dimension_semantics=("parallel", "parallel"), vmem_limit_bytes=VMEM_LIMIT_BYTES),
        name="xattn",
    )(h, g, w_q, kt, v, w_o)


def _ffn_kernel(h_ref, g_ref, wgu_ref, wd_ref, gf_ref, out_ref, a_s, *, final_norm):
    x = h_ref[...]
    u = _rms(x, g_ref[...]).astype(BF16)
    for ci in range(D_FF // FF_COL_CHUNK):
        lo = ci * FF_COL_CHUNK
        gate = _dot(u, wgu_ref[:, lo:lo + FF_COL_CHUNK])
        up = _dot(u, wgu_ref[:, D_FF + lo:D_FF + lo + FF_COL_CHUNK])
        a_s[:, lo:lo + FF_COL_CHUNK] = (_silu(gate) * up).astype(BF16)
    y = x + _dot(a_s[...], wd_ref[...])
    if final_norm:
        y = _rms(y, gf_ref[...])
    out_ref[...] = y


def _ffn(h2, layer, g, w_gate_up, w_down, g_final, final_norm):
    t = h2.shape[0]
    tm = FFN_BLOCK
    return pl.pallas_call(
        functools.partial(_ffn_kernel, final_norm=final_norm),
        grid=(t // tm,),
        in_specs=[pl.BlockSpec((tm, D_MODEL), lambda i: (i, 0)),
                  _layer_spec(g.shape, layer), _layer_spec(w_gate_up.shape, layer),
                  _layer_spec(w_down.shape, layer), _const_spec((1, D_MODEL))],
        out_specs=pl.BlockSpec((tm, D_MODEL), lambda i: (i, 0)),
        out_shape=jax.ShapeDtypeStruct(h2.shape, F32),
        scratch_shapes=[pltpu.VMEM((tm, D_FF), BF16)],
        compiler_params=pltpu.CompilerParams(
            dimension_semantics=("parallel",), vmem_limit_bytes=VMEM_LIMIT_BYTES),
        name="swiglu_final" if final_norm else "swiglu",
    )(h2, g, w_gate_up, w_down, g_final.reshape(1, -1))


def kernel(x, mem, mix_norm, w_in, conv_w, conv_b, dt_bias, a_log, d_skip, ssd_norm, pool_w, pool_scale,
           w_out_mix, xattn_norm, mem_norm, w_q, w_kv, w_o, ffn_norm, w_gate_up, w_down, final_norm):
    bsz, s, d = x.shape
    depth = w_in.shape[0]
    row = lambda a: a.reshape(depth, 1, -1)
    mixer_params = _prep_mixer_params(mix_norm, w_in, conv_w, conv_b, dt_bias, a_log, d_skip, ssd_norm, pool_w,
                                      pool_scale, w_out_mix)
    xattn_g, mem_g, ffn_g = row(xattn_norm), row(mem_norm), row(ffn_norm)
    wq_b, wkv_b, wo_b = w_q.astype(BF16), w_kv.astype(BF16), w_o.astype(BF16)
    wgu_b, wd_b = w_gate_up.astype(BF16), w_down.astype(BF16)
    h = x
    for l in range(depth):
        h = _mixer(h, l, mixer_params)
        kt, v = _kv(mem, l, mem_g, wkv_b)
        h = _xattn(h, l, xattn_g, wq_b, kt, v, wo_b)
        h = _ffn(h.reshape(bsz * s, d), l, ffn_g, wgu_b, wd_b, final_norm,
                 final_norm=(l == depth - 1)).reshape(bsz, s, d)
    return h
```

```python
import functools

import numpy as np
import jax
import jax.numpy as jnp
from jax import lax
from jax.experimental import pallas as pl
from jax.experimental.pallas import tpu as pltpu

F32 = jnp.float32
BF16 = jnp.bfloat16

D_MODEL = 1024
SSD_INNER = 1024
SSD_HEADS = 16
SSD_HEAD_DIM = 64
SSD_GROUPS = 2
SSD_STATE = 128
GROUP_WIDTH = SSD_INNER // SSD_GROUPS
CONV_WIDTH = 4
CONV_DIM = SSD_INNER + 2 * SSD_GROUPS * SSD_STATE
CHUNK = 128
POOL_WINDOWS = (2, 4, 8, 16)
POOL_WIDTH = 1024
POOL_GROUP_DIM = 256
D_MIX = 2048
DT_LANES = 128
DT_COPIES = 3
MEM_LEN = 256
XATTN_HEADS = 4
XATTN_HEAD_DIM = 256
D_FF = 2816
FF_COL_CHUNK = 704
EPS = 1e-6
LOG2_E = 1.4426950408889634

ROW_MIX_NORM, ROW_CONV_W, ROW_CONV_B, ROW_DT_BIAS, ROW_A_LOG, ROW_D_SKIP, ROW_SSD_NORM, ROW_POOL_SCALE = (
    0, 1, 5, 6, 7, 8, 9, 10)
ROW_PARAM_ROWS = 16
CST_TRI = 0
CST_E64 = CST_TRI + DT_COPIES * CHUNK
CST_E128 = CST_E64 + SSD_INNER
CST_BAND = CST_E128 + SSD_HEADS * CHUNK

WEIGHT_PAD_COLS = 128
CONV_HALO = 8
CONV_COLS = 256
SEQ_BLOCK = 512
FFN_BLOCK = 1024
XATTN_BLOCK = 1024
VMEM_LIMIT_BYTES = 56 * 1024 * 1024


def _rms(x, g):
    ms = jnp.mean(x * x, axis=-1, keepdims=True)
    return x * lax.rsqrt(ms + EPS) * g


def _silu(x):
    h = 0.5 * x
    return h + h * jnp.tanh(h)


def _softplus(x):
    return jnp.maximum(x, 0.0) + jnp.log1p(jnp.exp(-jnp.abs(x)))


def _dot(a, b):
    return jnp.dot(a, b, preferred_element_type=F32)


def _split3_f32(x):
    hi = x.astype(BF16).astype(F32)
    r1 = x - hi
    mid = r1.astype(BF16).astype(F32)
    return hi, mid, r1 - mid


def _pack3(x, lane):
    hi, mid, lo = _split3_f32(x)
    return jnp.where(lane < SSD_HEADS, hi, jnp.where(lane < 2 * SSD_HEADS, mid, lo)).astype(BF16)


def _mixer_kernel(h_ref, rowp_ref, wzx_ref, wv_ref, wdt_ref, wout_ref, poolw_ref, cst_ref,
                  out_ref,
                  z_s, halo_s, xbcs_s, vf_s, vb_s, state_s, acst_s, pk_s, eacs_s, xdec_s, xlo_s, xhi_s,
                  bmt_s, cmb_s, cb_s, y_s, yoff_s, ycat_s, acc_s):
    ts = z_s.shape[0]
    nch = ts // CHUNK
    j = pl.program_id(1)

    @pl.when(j == 0)
    def _():
        state_s[...] = jnp.zeros_like(state_s)
        halo_s[...] = jnp.zeros_like(halo_s)
        vb_s[0:CHUNK, :] = jnp.zeros((CHUNK, POOL_WIDTH), BF16)

    def rows(c):
        return slice(c * CHUNK, (c + 1) * CHUNK)

    g_ref = rowp_ref.at[ROW_MIX_NORM:ROW_MIX_NORM + 1, 0:D_MODEL]
    convw_ref = rowp_ref.at[ROW_CONV_W:ROW_CONV_W + CONV_WIDTH, :]
    convb_ref = rowp_ref.at[ROW_CONV_B:ROW_CONV_B + 1, :]
    dtb_ref = rowp_ref.at[ROW_DT_BIAS:ROW_DT_BIAS + 1, 0:DT_LANES]
    alog_ref = rowp_ref.at[ROW_A_LOG:ROW_A_LOG + 1, 0:DT_LANES]
    dskip_ref = rowp_ref.at[ROW_D_SKIP:ROW_D_SKIP + 1, 0:SSD_INNER]
    ssdn_ref = rowp_ref.at[ROW_SSD_NORM:ROW_SSD_NORM + 1, 0:SSD_INNER]
    pscale_ref = rowp_ref.at[ROW_POOL_SCALE:ROW_POOL_SCALE + 1, 0:POOL_WIDTH]
    tri_ref = cst_ref.at[:, CST_TRI:CST_TRI + DT_COPIES * CHUNK]
    e64_ref = cst_ref.at[:, CST_E64:CST_E64 + SSD_INNER]
    e128_ref = cst_ref.at[:, CST_E128:CST_E128 + SSD_HEADS * CHUNK]

    def band(gi):
        return cst_ref[:, CST_BAND + gi * 2 * CHUNK:CST_BAND + (gi + 1) * 2 * CHUNK]

    x = h_ref[0]
    u = _rms(x, g_ref[...]).astype(BF16)

    def z_piece(i):
        z_s[:, i * CONV_COLS:(i + 1) * CONV_COLS] = _dot(u, wzx_ref[:, i * CONV_COLS:(i + 1) * CONV_COLS])

    def v_piece(i):
        v = _dot(u, wv_ref[:, i * CONV_COLS:(i + 1) * CONV_COLS])
        vf_s[:, i * CONV_COLS:(i + 1) * CONV_COLS] = v
        vb_s[CHUNK:CHUNK + ts, i * CONV_COLS:(i + 1) * CONV_COLS] = v.astype(BF16)

    def conv_group(cg):
        cols = slice(cg * CONV_COLS, (cg + 1) * CONV_COLS)
        res = _dot(u, wzx_ref[:, SSD_INNER + cg * CONV_COLS:SSD_INNER + (cg + 1) * CONV_COLS])
        w = [convw_ref[k:k + 1, cols] for k in range(CONV_WIDTH)]
        ext = jnp.concatenate([halo_s[:, cols], res], axis=0)
        prev = pltpu.roll(ext, 1, axis=0)
        b = w[1] * ext + w[0] * prev
        a = convb_ref[:, cols] + w[3] * res + w[2] * prev[CONV_HALO:, :]
        xbcs_s[:, cols] = _silu(a + pltpu.roll(b, 2, axis=0)[CONV_HALO:, :])
        halo_s[:, cols] = res[ts - CONV_HALO:ts, :]

    lane = lax.broadcasted_iota(jnp.int32, (CHUNK, CHUNK), 1)
    row = lax.broadcasted_iota(jnp.int32, (CHUNK, CHUNK), 0)
    causal = row >= lane
    lane_b = lax.broadcasted_iota(jnp.int32, (ts, DT_LANES), 1)
    a_row = jnp.where(lane[0:1, :] < DT_COPIES * SSD_HEADS, -jnp.exp(alog_ref[...]), 0.0)

    tpos0 = j * ts + 1 + lax.broadcasted_iota(jnp.int32, (CHUNK, POOL_GROUP_DIM), 0)

    def pool_group(gi):
        w = POOL_WINDOWS[gi]
        cols = slice(gi * POOL_GROUP_DIM, (gi + 1) * POOL_GROUP_DIM)
        ds = []
        for c in range(nch):
            win = _dot(band(gi), vb_s[c * CHUNK:(c + 2) * CHUNK, cols])
            if c == 0:
                mean = win / jnp.minimum(tpos0.astype(F32), float(w))
            else:
                mean = win * (1.0 / w)
            ds.append((mean - vf_s[rows(c), cols]).astype(BF16))
        yp = _dot(jnp.concatenate(ds, axis=0), poolw_ref[gi]) * pscale_ref[:, cols]
        ycat_s[:, SSD_INNER + gi * POOL_GROUP_DIM:SSD_INNER + (gi + 1) * POOL_GROUP_DIM] = yp.astype(BF16)

    def pool_out_piece(k):
        cols = slice(k * CONV_COLS, (k + 1) * CONV_COLS)
        acc_s[:, cols] = x[:, cols] + _dot(ycat_s[:, SSD_INNER:D_MIX], wout_ref[SSD_INNER:D_MIX, cols])

    n_side = SSD_INNER // CONV_COLS
    fillers = ([functools.partial(v_piece, k) for k in range(n_side)]
               + [functools.partial(pool_group, gi) for gi in range(len(POOL_WINDOWS))]
               + [functools.partial(pool_out_piece, k) for k in range(n_side)]
               + [functools.partial(z_piece, k) for k in range(n_side)])

    def fill(n):
        for _ in range(n):
            if fillers:
                fillers.pop(0)()

    dt = _softplus(_dot(u, wdt_ref[...]) + dtb_ref[...])
    adt_parts = _split3_f32(dt * a_row)
    fill(2)
    conv_group(0)
    acs_c = []
    for c in range(nch):
        stacked = jnp.concatenate([p[rows(c)] for p in adt_parts], axis=0).astype(BF16)
        acs_c.append(_dot(tri_ref[...], stacked) * LOG2_E)
    fill(2)
    conv_group(1)
    for c in range(nch):
        acst_s[c] = acs_c[c].T
    pk_acs = _pack3(jnp.concatenate(acs_c, axis=0), lane_b)
    pk_s[...] = pk_acs
    pk_dt = _pack3(dt, lane_b)
    vb_tail = vb_s[ts:ts + CHUNK, :]
    fill(1)
    conv_group(2)
    acs_x = _dot(pk_acs, e64_ref[...])
    dt_x = _dot(pk_dt, e64_ref[...])
    fill(1)
    conv_group(3)
    eacs_s[...] = jnp.exp2(acs_x)
    fill(2)
    lane_x = lax.broadcasted_iota(jnp.int32, (CHUNK, SSD_INNER), 1) % (2 * SSD_HEAD_DIM)
    for c in range(nch):
        r = rows(c)
        xc = xbcs_s[r, 0:SSD_INNER] * dt_x[r]
        alast = acs_x[(c + 1) * CHUNK - 1:(c + 1) * CHUNK, :]
        xdec_s[r, :] = (xc * jnp.exp2(alast - acs_x[r])).astype(BF16)
        xlo_s[r, :] = jnp.where(lane_x < SSD_HEAD_DIM, xc, 0.0).astype(BF16)
        xhi_s[r, :] = jnp.where(lane_x >= SSD_HEAD_DIM, xc, 0.0).astype(BF16)
        fill(1)
        if c == 1:
            conv_group(4)
    conv_group(5)
    fill(1)
    for c in range(nch):
        for g in range(SSD_GROUPS):
            b_cols = slice(SSD_INNER + g * SSD_STATE, SSD_INNER + (g + 1) * SSD_STATE)
            c_cols = slice(SSD_INNER + (SSD_GROUPS + g) * SSD_STATE, SSD_INNER + (SSD_GROUPS + g + 1) * SSD_STATE)
            bmt_s[c, g] = xbcs_s[rows(c), b_cols].T.astype(BF16)
            cmb_s[c, g] = xbcs_s[rows(c), c_cols].astype(BF16)

    pairs_per_group = SSD_HEADS // SSD_GROUPS // 2
    for p in range(SSD_HEADS // 2):
        g = p // pairs_per_group
        pcols = slice(p * 2 * SSD_HEAD_DIM, (p + 1) * 2 * SSD_HEAD_DIM)
        if p % pairs_per_group == 0:
            for c in range(nch):
                cb_s[c] = _dot(cmb_s[c, g], bmt_s[c, g])
        cpair = _dot(pk_s[...], e128_ref[:, p * 2 * CHUNK:(p + 1) * 2 * CHUNK])
        for c in range(nch):
            ms = []
            for q in range(2):
                hh = 2 * p + q
                dmat = cpair[rows(c), q * CHUNK:(q + 1) * CHUNK] - acst_s[c, hh:hh + 1, :]
                seg = jnp.exp2(jnp.where(causal, dmat, -jnp.inf))
                ms.append((cb_s[c] * seg).astype(BF16))
            rhs = jnp.concatenate([xlo_s[rows(c), pcols], xhi_s[rows(c), pcols]], axis=0)
            y_s[rows(c), pcols] = _dot(jnp.concatenate(ms, axis=1), rhs)
        if p < 1:
            fill(1)
    vb_s[0:CHUNK, :] = vb_tail

    new_st = {}
    for c in range(nch):
        for g in range(SSD_GROUPS):
            gcols = slice(g * GROUP_WIDTH, (g + 1) * GROUP_WIDTH)
            new_st[c, g] = _dot(bmt_s[c, g], xdec_s[rows(c), gcols])
    fill(len(fillers))
    for c in range(nch):
        for g in range(SSD_GROUPS):
            gcols = slice(g * GROUP_WIDTH, (g + 1) * GROUP_WIDTH)
            st = state_s[g]
            yoff_s[rows(c), gcols] = _dot(cmb_s[c, g], st.astype(BF16))
            state_s[g] = st * eacs_s[(c + 1) * CHUNK - 1:(c + 1) * CHUNK, gcols] + new_st[c, g]

    for c in range(nch):
        r = rows(c)
        y = y_s[r, :] + yoff_s[r, :] * eacs_s[r, :] + dskip_ref[...] * xbcs_s[r, 0:SSD_INNER]
        y = y * _silu(z_s[r, :])
        ycat_s[r, 0:SSD_INNER] = _rms(y, ssdn_ref[...]).astype(BF16)
        out_ref[0, r, :] = acc_s[r, :] + _dot(ycat_s[r, 0:SSD_INNER], wout_ref[0:SSD_INNER, 0:D_MODEL])


def _const_spec(shape):
    nd = len(shape)
    return pl.BlockSpec(shape, lambda *_: (0,) * nd, pipeline_mode=pl.Buffered(1))


def _pad_cols(w):
    return jnp.pad(w, ((0, 0), (0, 0), (0, WEIGHT_PAD_COLS)))


def _layer_spec(shape, layer):
    nd = len(shape)
    return pl.BlockSpec((None,) + tuple(shape[1:]), lambda *_: (layer,) + (0,) * (nd - 1),
                        pipeline_mode=pl.Buffered(1))


def _mixer_constants():
    t = np.arange(CHUNK)
    tri = (t[:, None] >= t[None, :]).astype(np.float32)
    tri3 = np.concatenate([tri] * DT_COPIES, axis=1)
    e64 = np.zeros((DT_LANES, SSD_INNER), np.float32)
    e128 = np.zeros((DT_LANES, SSD_HEADS * CHUNK), np.float32)
    for c in range(DT_COPIES):
        for h in range(SSD_HEADS):
            e64[c * SSD_HEADS + h, h * SSD_HEAD_DIM:(h + 1) * SSD_HEAD_DIM] = 1.0
            e128[c * SSD_HEADS + h, h * CHUNK:(h + 1) * CHUNK] = 1.0
    k = np.arange(2 * CHUNK) - CHUNK
    band = [((k[None, :] <= t[:, None]) & (k[None, :] >= t[:, None] - w + 1)).astype(np.float32)
            for w in POOL_WINDOWS]
    return jnp.asarray(np.concatenate([tri3, e64, e128] + band, axis=1), BF16)


def _prep_mixer_params(mix_norm, w_in, conv_w, conv_b, dt_bias, a_log, d_skip, ssd_norm, pool_w, pool_scale,
                       w_out_mix):
    depth = w_in.shape[0]
    xbc_end = SSD_INNER + CONV_DIM
    dt_end = xbc_end + SSD_HEADS
    w_dt = w_in[:, :, xbc_end:dt_end]
    n_pad = DT_LANES - DT_COPIES * SSD_HEADS
    w_dt3 = jnp.concatenate([w_dt] * DT_COPIES + [jnp.zeros((depth, D_MODEL, n_pad), F32)], axis=2).astype(BF16)
    w_zx = w_in[:, :, :xbc_end].astype(BF16)
    w_v = _pad_cols(w_in[:, :, dt_end:].astype(BF16))
    pad = jnp.zeros((depth, n_pad), F32)

    def row(a):
        return jnp.pad(a, ((0, 0), (0, CONV_DIM - a.shape[1])))[:, None, :]

    rowp = jnp.concatenate([
        row(mix_norm), conv_w, row(conv_b),
        row(jnp.concatenate([dt_bias] * DT_COPIES + [pad], axis=1)),
        row(jnp.concatenate([a_log] * DT_COPIES + [pad], axis=1)),
        row(jnp.repeat(d_skip, SSD_HEAD_DIM, axis=1)), row(ssd_norm), row(pool_scale),
        jnp.zeros((depth, ROW_PARAM_ROWS - ROW_POOL_SCALE - 1, CONV_DIM), F32)], axis=1)
    return rowp, w_zx, w_v, w_dt3, _pad_cols(w_out_mix.astype(BF16)), pool_w.astype(BF16)


def _mixer(h, layer, params):
    bsz, s, _ = h.shape
    ts = SEQ_BLOCK
    nch = ts // CHUNK
    consts = _mixer_constants()
    in_specs = [pl.BlockSpec((1, ts, D_MODEL), lambda b, j: (b, j, 0))]
    in_specs += [_layer_spec(a.shape, layer) for a in params]
    in_specs += [_const_spec(consts.shape)]
    return pl.pallas_call(
        _mixer_kernel,
        grid=(bsz, s // ts),
        in_specs=in_specs,
        out_specs=pl.BlockSpec((1, ts, D_MODEL), lambda b, j: (b, j, 0)),
        out_shape=jax.ShapeDtypeStruct(h.shape, F32),
        scratch_shapes=[
            pltpu.VMEM((ts, SSD_INNER), F32),
            pltpu.VMEM((CONV_HALO, CONV_DIM), F32),
            pltpu.VMEM((ts, CONV_DIM), F32),
            pltpu.VMEM((ts, POOL_WIDTH), F32),
            pltpu.VMEM((CHUNK + ts, POOL_WIDTH), BF16),
            pltpu.VMEM((SSD_GROUPS, SSD_STATE, GROUP_WIDTH), F32),
            pltpu.VMEM((nch, CHUNK, CHUNK), F32),
            pltpu.VMEM((ts, DT_LANES), BF16),
            pltpu.VMEM((ts, SSD_INNER), F32),
            pltpu.VMEM((ts, SSD_INNER), BF16),
            pltpu.VMEM((ts, SSD_INNER), BF16),
            pltpu.VMEM((ts, SSD_INNER), BF16),
            pltpu.VMEM((nch, SSD_GROUPS, SSD_STATE, CHUNK), BF16),
            pltpu.VMEM((nch, SSD_GROUPS, CHUNK, SSD_STATE), BF16),
            pltpu.VMEM((nch, CHUNK, CHUNK), F32),
            pltpu.VMEM((ts, SSD_INNER), F32),
            pltpu.VMEM((ts, SSD_INNER), F32),
            pltpu.VMEM((ts, D_MIX), BF16),
            pltpu.VMEM((ts, D_MODEL), F32),
        ],
        compiler_params=pltpu.CompilerParams(
            dimension_semantics=("parallel", "arbitrary"), vmem_limit_bytes=VMEM_LIMIT_BYTES),
        name="mixer",
    )(h, *params, consts)


def _kv_kernel(mem_ref, g_ref, wkv_ref, kt_ref, v_ref):
    mn = _rms(mem_ref[0], g_ref[...]).astype(BF16)
    kv = _dot(mn, wkv_ref[:, 0:2 * D_MODEL])
    kt_ref[0] = kv[:, 0:D_MODEL].T.astype(BF16)
    v_ref[0] = kv[:, D_MODEL:2 * D_MODEL].astype(BF16)


def _kv(mem, layer, g, w_kv):
    bsz = mem.shape[0]
    return pl.pallas_call(
        _kv_kernel,
        grid=(bsz,),
        in_specs=[pl.BlockSpec((1, MEM_LEN, D_MODEL), lambda b: (b, 0, 0)),
                  _layer_spec(g.shape, layer), _layer_spec(w_kv.shape, layer)],
        out_specs=[pl.BlockSpec((1, D_MODEL, MEM_LEN), lambda b: (b, 0, 0)),
                   pl.BlockSpec((1, MEM_LEN, D_MODEL), lambda b: (b, 0, 0))],
        out_shape=[jax.ShapeDtypeStruct((bsz, D_MODEL, MEM_LEN), BF16),
                   jax.ShapeDtypeStruct((bsz, MEM_LEN, D_MODEL), BF16)],
        compiler_params=pltpu.CompilerParams(
            dimension_semantics=("parallel",), vmem_limit_bytes=VMEM_LIMIT_BYTES),
        name="memory_kv",
    )(mem, g, w_kv)


def _xattn_kernel(h_ref, g_ref, wq_ref, kt_ref, v_ref, wo_ref, out_ref, o_s):
    x = h_ref[0]
    u = _rms(x, g_ref[...]).astype(BF16)
    q = (_dot(u, wq_ref[:, 0:D_MODEL]) * (XATTN_HEAD_DIM ** -0.5)).astype(BF16)
    head_cols = [slice(hd * XATTN_HEAD_DIM, (hd + 1) * XATTN_HEAD_DIM) for hd in range(XATTN_HEADS)]
    scores = [_dot(q[:, cols], kt_ref[0, cols, :]) for cols in head_cols]
    for cols, s in zip(head_cols, scores):
        p = jnp.exp(s - jnp.max(s, axis=-1, keepdims=True))
        den = jnp.sum(p, axis=-1, keepdims=True)
        o = _dot(p.astype(BF16), v_ref[0, :, cols])
        o_s[:, cols] = (o * (1.0 / den)).astype(BF16)
    out_ref[0] = x + _dot(o_s[...], wo_ref[:, 0:D_MODEL])


def _xattn(h, layer, g, w_q, kt, v, w_o):
    bsz, s, _ = h.shape
    tq = XATTN_BLOCK
    return pl.pallas_call(
        _xattn_kernel,
        grid=(bsz, s // tq),
        in_specs=[pl.BlockSpec((1, tq, D_MODEL), lambda b, j: (b, j, 0)),
                  _layer_spec(g.shape, layer), _layer_spec(w_q.shape, layer),
                  pl.BlockSpec((1, D_MODEL, MEM_LEN), lambda b, j: (b, 0, 0)),
                  pl.BlockSpec((1, MEM_LEN, D_MODEL), lambda b, j: (b, 0, 0)),
                  _layer_spec(w_o.shape, layer)],
        out_specs=pl.BlockSpec((1, tq, D_MODEL), lambda b, j: (b, j, 0)),
        out_shape=jax.ShapeDtypeStruct(h.shape, F32),
        scratch_shapes=[pltpu.VMEM((tq, D_MODEL), BF16)],
        compiler_params=pltpu.CompilerParams(
            dimension_semantics=("parallel", "parallel"), vmem_limit_bytes=VMEM_LIMIT_BYTES),
        name="xattn",
    )(h, g, w_q, kt, v, w_o)


def _ffn_kernel(h_ref, g_ref, wgu_ref, wd_ref, gf_ref, out_ref, a_s, *, final_norm):
    x = h_ref[...]
    u = _rms(x, g_ref[...]).astype(BF16)
    for ci in range(D_FF // FF_COL_CHUNK):
        lo = ci * FF_COL_CHUNK
        gate = _dot(u, wgu_ref[:, lo:lo + FF_COL_CHUNK])
        up = _dot(u, wgu_ref[:, D_FF + lo:D_FF + lo + FF_COL_CHUNK])
        a_s[:, lo:lo + FF_COL_CHUNK] = (_silu(gate) * up).astype(BF16)
    y = x + _dot(a_s[...], wd_ref[:, 0:D_MODEL])
    if final_norm:
        y = _rms(y, gf_ref[...])
    out_ref[...] = y


def _ffn(h2, layer, g, w_gate_up, w_down, g_final, final_norm):
    t = h2.shape[0]
    tm = FFN_BLOCK
    return pl.pallas_call(
        functools.partial(_ffn_kernel, final_norm=final_norm),
        grid=(t // tm,),
        in_specs=[pl.BlockSpec((tm, D_MODEL), lambda i: (i, 0)),
                  _layer_spec(g.shape, layer), _layer_spec(w_gate_up.shape, layer),
                  _layer_spec(w_down.shape, layer), _const_spec((1, D_MODEL))],
        out_specs=pl.BlockSpec((tm, D_MODEL), lambda i: (i, 0)),
        out_shape=jax.ShapeDtypeStruct(h2.shape, F32),
        scratch_shapes=[pltpu.VMEM((tm, D_FF), BF16)],
        compiler_params=pltpu.CompilerParams(
            dimension_semantics=("parallel",), vmem_limit_bytes=VMEM_LIMIT_BYTES),
        name="swiglu_final" if final_norm else "swiglu",
    )(h2, g, w_gate_up, w_down, g_final.reshape(1, -1))


def kernel(x, mem, mix_norm, w_in, conv_w, conv_b, dt_bias, a_log, d_skip, ssd_norm, pool_w, pool_scale,
           w_out_mix, xattn_norm, mem_norm, w_q, w_kv, w_o, ffn_norm, w_gate_up, w_down, final_norm):
    bsz, s, d = x.shape
    depth = w_in.shape[0]
    row = lambda a: a.reshape(depth, 1, -1)
    mixer_params = _prep_mixer_params(mix_norm, w_in, conv_w, conv_b, dt_bias, a_log, d_skip, ssd_norm, pool_w,
                                      pool_scale, w_out_mix)
    xattn_g, mem_g, ffn_g = row(xattn_norm), row(mem_norm), row(ffn_norm)
    wq_b, wkv_b, wo_b = (_pad_cols(w.astype(BF16)) for w in (w_q, w_kv, w_o))
    wgu_b, wd_b = w_gate_up.astype(BF16), _pad_cols(w_down.astype(BF16))
    h = x
    for l in range(depth):
        h = _mixer(h, l, mixer_params)
        kt, v = _kv(mem, l, mem_g, wkv_b)
        h = _xattn(h, l, xattn_g, wq_b, kt, v, wo_b)
        h = _ffn(h.reshape(bsz * s, d), l, ffn_g, wgu_b, wd_b, final_norm,
                 final_norm=(l == depth - 1)).reshape(bsz, s, d)
    return h
```

```python
import functools

import numpy as np
import jax
import jax.numpy as jnp
from jax import lax
from jax.experimental import pallas as pl
from jax.experimental.pallas import tpu as pltpu

F32 = jnp.float32
BF16 = jnp.bfloat16

D_MODEL = 1024
SSD_INNER = 1024
SSD_HEADS = 16
SSD_HEAD_DIM = 64
SSD_GROUPS = 2
SSD_STATE = 128
GROUP_WIDTH = SSD_INNER // SSD_GROUPS
CONV_WIDTH = 4
CONV_DIM = SSD_INNER + 2 * SSD_GROUPS * SSD_STATE
CHUNK = 128
POOL_WINDOWS = (2, 4, 8, 16)
POOL_WIDTH = 1024
POOL_GROUP_DIM = 256
D_MIX = 2048
DT_LANES = 128
DT_COPIES = 3
IN_COLS = SSD_INNER + CONV_DIM + POOL_WIDTH + DT_LANES
MEM_LEN = 256
XATTN_HEADS = 4
XATTN_HEAD_DIM = 256
D_FF = 2816
FF_COL_CHUNK = 1408
EPS = 1e-6
LOG2_E = 1.4426950408889634

ROW_MIX_NORM, ROW_CONV_W, ROW_CONV_B, ROW_DT_BIAS, ROW_A_LOG, ROW_D_SKIP, ROW_SSD_NORM, ROW_POOL_SCALE = (
    0, 1, 5, 6, 7, 8, 9, 10)
ROW_PARAM_ROWS = 16
CST_TRI = 0
CST_E64 = CST_TRI + DT_COPIES * CHUNK
CST_E128 = CST_E64 + SSD_INNER
CST_BAND = CST_E128 + SSD_HEADS * CHUNK

CONV_HALO = 8
CONV_COLS = 256
SEQ_BLOCK = 512
VMEM_LIMIT_BYTES = 56 * 1024 * 1024


def _rms(x, g):
    ms = jnp.mean(x * x, axis=-1, keepdims=True)
    return x * lax.rsqrt(ms + EPS) * g


def _silu(x):
    h = 0.5 * x
    return h + h * jnp.tanh(h)


def _softplus(x):
    return jnp.maximum(x, 0.0) + jnp.log1p(jnp.exp(-jnp.abs(x)))


def _dot(a, b):
    return jnp.dot(a, b, preferred_element_type=F32)


def _split3_f32(x):
    hi = x.astype(BF16).astype(F32)
    r1 = x - hi
    mid = r1.astype(BF16).astype(F32)
    return hi, mid, r1 - mid


def _pack3(x, lane):
    hi, mid, lo = _split3_f32(x)
    return jnp.where(lane < SSD_HEADS, hi, jnp.where(lane < 2 * SSD_HEADS, mid, lo)).astype(BF16)


def _mixer_kernel(h_ref, rowp_ref, win_ref, wout_ref, poolw_ref, cst_ref,
                  out_ref,
                  z_s, halo_s, xbcs_s, vf_s, vb_s, state_s, acst_s, pk_s, eacs_s, xdec_s, xlo_s, xhi_s,
                  bmt_s, cmb_s, cb_s, y_s, yoff_s, ycat_s, acc_s):
    ts = z_s.shape[0]
    nch = ts // CHUNK
    j = pl.program_id(1)

    @pl.when(j == 0)
    def _():
        state_s[...] = jnp.zeros_like(state_s)
        halo_s[...] = jnp.zeros_like(halo_s)
        vb_s[0:CHUNK, :] = jnp.zeros((CHUNK, POOL_WIDTH), BF16)

    def rows(c):
        return slice(c * CHUNK, (c + 1) * CHUNK)

    g_ref = rowp_ref.at[ROW_MIX_NORM:ROW_MIX_NORM + 1, 0:D_MODEL]
    convw_ref = rowp_ref.at[ROW_CONV_W:ROW_CONV_W + CONV_WIDTH, :]
    convb_ref = rowp_ref.at[ROW_CONV_B:ROW_CONV_B + 1, :]
    dtb_ref = rowp_ref.at[ROW_DT_BIAS:ROW_DT_BIAS + 1, 0:DT_LANES]
    alog_ref = rowp_ref.at[ROW_A_LOG:ROW_A_LOG + 1, 0:DT_LANES]
    dskip_ref = rowp_ref.at[ROW_D_SKIP:ROW_D_SKIP + 1, 0:SSD_INNER]
    ssdn_ref = rowp_ref.at[ROW_SSD_NORM:ROW_SSD_NORM + 1, 0:SSD_INNER]
    pscale_ref = rowp_ref.at[ROW_POOL_SCALE:ROW_POOL_SCALE + 1, 0:POOL_WIDTH]
    tri_ref = cst_ref.at[:, CST_TRI:CST_TRI + DT_COPIES * CHUNK]
    e64_ref = cst_ref.at[:, CST_E64:CST_E64 + SSD_INNER]
    e128_ref = cst_ref.at[:, CST_E128:CST_E128 + SSD_HEADS * CHUNK]

    def band(gi):
        return cst_ref[:, CST_BAND + gi * 2 * CHUNK:CST_BAND + (gi + 1) * 2 * CHUNK]

    x = h_ref[0]
    u = _rms(x, g_ref[...]).astype(BF16)

    v_off = SSD_INNER + CONV_DIM

    def z_piece(i):
        z_s[:, i * CONV_COLS:(i + 1) * CONV_COLS] = _dot(u, win_ref[:, i * CONV_COLS:(i + 1) * CONV_COLS])

    def v_piece(i):
        v = _dot(u, win_ref[:, v_off + i * CONV_COLS:v_off + (i + 1) * CONV_COLS])
        vf_s[:, i * CONV_COLS:(i + 1) * CONV_COLS] = v
        vb_s[CHUNK:CHUNK + ts, i * CONV_COLS:(i + 1) * CONV_COLS] = v.astype(BF16)

    def conv_group(cg):
        cols = slice(cg * CONV_COLS, (cg + 1) * CONV_COLS)
        res = _dot(u, win_ref[:, SSD_INNER + cg * CONV_COLS:SSD_INNER + (cg + 1) * CONV_COLS])
        w = [convw_ref[k:k + 1, cols] for k in range(CONV_WIDTH)]
        ext = jnp.concatenate([halo_s[:, cols], res], axis=0)
        prev = pltpu.roll(ext, 1, axis=0)
        b = w[1] * ext + w[0] * prev
        a = convb_ref[:, cols] + w[3] * res + w[2] * prev[CONV_HALO:, :]
        xbcs_s[:, cols] = _silu(a + pltpu.roll(b, 2, axis=0)[CONV_HALO:, :])
        halo_s[:, cols] = res[ts - CONV_HALO:ts, :]

    lane = lax.broadcasted_iota(jnp.int32, (CHUNK, CHUNK), 1)
    row = lax.broadcasted_iota(jnp.int32, (CHUNK, CHUNK), 0)
    causal = row >= lane
    lane_b = lax.broadcasted_iota(jnp.int32, (ts, DT_LANES), 1)
    a_row = jnp.where(lane[0:1, :] < DT_COPIES * SSD_HEADS, -jnp.exp(alog_ref[...]), 0.0)

    tpos0 = j * ts + 1 + lax.broadcasted_iota(jnp.int32, (CHUNK, POOL_GROUP_DIM), 0)

    def pool_group(gi):
        w = POOL_WINDOWS[gi]
        cols = slice(gi * POOL_GROUP_DIM, (gi + 1) * POOL_GROUP_DIM)
        ds = []
        for c in range(nch):
            win = _dot(band(gi), vb_s[c * CHUNK:(c + 2) * CHUNK, cols])
            if c == 0:
                mean = win / jnp.minimum(tpos0.astype(F32), float(w))
            else:
                mean = win * (1.0 / w)
            ds.append((mean - vf_s[rows(c), cols]).astype(BF16))
        yp = _dot(jnp.concatenate(ds, axis=0), poolw_ref[gi]) * pscale_ref[:, cols]
        ycat_s[:, SSD_INNER + gi * POOL_GROUP_DIM:SSD_INNER + (gi + 1) * POOL_GROUP_DIM] = yp.astype(BF16)

    def pool_out_piece(k):
        cols = slice(k * CONV_COLS, (k + 1) * CONV_COLS)
        acc_s[:, cols] = x[:, cols] + _dot(ycat_s[:, SSD_INNER:D_MIX], wout_ref[SSD_INNER:D_MIX, cols])

    n_side = SSD_INNER // CONV_COLS
    fillers = ([functools.partial(v_piece, k) for k in range(n_side)]
               + [functools.partial(pool_group, gi) for gi in range(len(POOL_WINDOWS))]
               + [functools.partial(pool_out_piece, k) for k in range(n_side)]
               + [functools.partial(z_piece, k) for k in range(n_side)])

    def fill(n):
        for _ in range(n):
            if fillers:
                fillers.pop(0)()

    dt = _softplus(_dot(u, win_ref[:, IN_COLS - DT_LANES:IN_COLS]) + dtb_ref[...])
    adt_parts = _split3_f32(dt * a_row)
    fill(2)
    conv_group(0)
    acs_c = []
    for c in range(nch):
        stacked = jnp.concatenate([p[rows(c)] for p in adt_parts], axis=0).astype(BF16)
        acs_c.append(_dot(tri_ref[...], stacked) * LOG2_E)
    fill(2)
    conv_group(1)
    for c in range(nch):
        acst_s[c] = acs_c[c].T
    pk_acs = _pack3(jnp.concatenate(acs_c, axis=0), lane_b)
    pk_s[...] = pk_acs
    pk_dt = _pack3(dt, lane_b)
    vb_tail = vb_s[ts:ts + CHUNK, :]
    fill(1)
    conv_group(2)
    acs_x = _dot(pk_acs, e64_ref[...])
    dt_x = _dot(pk_dt, e64_ref[...])
    fill(1)
    conv_group(3)
    eacs_s[...] = jnp.exp2(acs_x)
    fill(2)
    lane_x = lax.broadcasted_iota(jnp.int32, (CHUNK, SSD_INNER), 1) % (2 * SSD_HEAD_DIM)
    for c in range(nch):
        r = rows(c)
        xc = xbcs_s[r, 0:SSD_INNER] * dt_x[r]
        alast = acs_x[(c + 1) * CHUNK - 1:(c + 1) * CHUNK, :]
        xdec_s[r, :] = (xc * jnp.exp2(alast - acs_x[r])).astype(BF16)
        xlo_s[r, :] = jnp.where(lane_x < SSD_HEAD_DIM, xc, 0.0).astype(BF16)
        xhi_s[r, :] = jnp.where(lane_x >= SSD_HEAD_DIM, xc, 0.0).astype(BF16)
        fill(1)
        if c == 1:
            conv_group(4)
    conv_group(5)
    fill(1)
    for c in range(nch):
        for g in range(SSD_GROUPS):
            b_cols = slice(SSD_INNER + g * SSD_STATE, SSD_INNER + (g + 1) * SSD_STATE)
            c_cols = slice(SSD_INNER + (SSD_GROUPS + g) * SSD_STATE, SSD_INNER + (SSD_GROUPS + g + 1) * SSD_STATE)
            bmt_s[c, g] = xbcs_s[rows(c), b_cols].T.astype(BF16)
            cmb_s[c, g] = xbcs_s[rows(c), c_cols].astype(BF16)

    pairs_per_group = SSD_HEADS // SSD_GROUPS // 2
    for p in range(SSD_HEADS // 2):
        g = p // pairs_per_group
        pcols = slice(p * 2 * SSD_HEAD_DIM, (p + 1) * 2 * SSD_HEAD_DIM)
        if p % pairs_per_group == 0:
            for c in range(nch):
                cb_s[c] = _dot(cmb_s[c, g], bmt_s[c, g])
        cpair = _dot(pk_s[...], e128_ref[:, p * 2 * CHUNK:(p + 1) * 2 * CHUNK])
        for c in range(nch):
            ms = []
            for q in range(2):
                hh = 2 * p + q
                dmat = cpair[rows(c), q * CHUNK:(q + 1) * CHUNK] - acst_s[c, hh:hh + 1, :]
                seg = jnp.exp2(jnp.where(causal, dmat, -jnp.inf))
                ms.append((cb_s[c] * seg).astype(BF16))
            rhs = jnp.concatenate([xlo_s[rows(c), pcols], xhi_s[rows(c), pcols]], axis=0)
            y_s[rows(c), pcols] = _dot(jnp.concatenate(ms, axis=1), rhs)
        if p < 1:
            fill(1)
    vb_s[0:CHUNK, :] = vb_tail

    new_st = {}
    for c in range(nch):
        for g in range(SSD_GROUPS):
            gcols = slice(g * GROUP_WIDTH, (g + 1) * GROUP_WIDTH)
            new_st[c, g] = _dot(bmt_s[c, g], xdec_s[rows(c), gcols])
    fill(len(fillers))
    for c in range(nch):
        for g in range(SSD_GROUPS):
            gcols = slice(g * GROUP_WIDTH, (g + 1) * GROUP_WIDTH)
            st = state_s[g]
            yoff_s[rows(c), gcols] = _dot(cmb_s[c, g], st.astype(BF16))
            state_s[g] = st * eacs_s[(c + 1) * CHUNK - 1:(c + 1) * CHUNK, gcols] + new_st[c, g]

    for c in range(nch):
        r = rows(c)
        y = y_s[r, :] + yoff_s[r, :] * eacs_s[r, :] + dskip_ref[...] * xbcs_s[r, 0:SSD_INNER]
        y = y * _silu(z_s[r, :])
        ycat_s[r, 0:SSD_INNER] = _rms(y, ssdn_ref[...]).astype(BF16)
        out_ref[0, r, :] = acc_s[r, :] + _dot(ycat_s[r, 0:SSD_INNER], wout_ref[0:SSD_INNER, :])


def _const_spec(shape):
    nd = len(shape)
    return pl.BlockSpec(shape, lambda *_: (0,) * nd, pipeline_mode=pl.Buffered(1))


def _layer_spec(shape, layer):
    nd = len(shape)
    return pl.BlockSpec((None,) + tuple(shape[1:]), lambda *_: (layer,) + (0,) * (nd - 1),
                        pipeline_mode=pl.Buffered(1))


def _mixer_constants():
    t = np.arange(CHUNK)
    tri = (t[:, None] >= t[None, :]).astype(np.float32)
    tri3 = np.concatenate([tri] * DT_COPIES, axis=1)
    e64 = np.zeros((DT_LANES, SSD_INNER), np.float32)
    e128 = np.zeros((DT_LANES, SSD_HEADS * CHUNK), np.float32)
    for c in range(DT_COPIES):
        for h in range(SSD_HEADS):
            e64[c * SSD_HEADS + h, h * SSD_HEAD_DIM:(h + 1) * SSD_HEAD_DIM] = 1.0
            e128[c * SSD_HEADS + h, h * CHUNK:(h + 1) * CHUNK] = 1.0
    k = np.arange(2 * CHUNK) - CHUNK
    band = [((k[None, :] <= t[:, None]) & (k[None, :] >= t[:, None] - w + 1)).astype(np.float32)
            for w in POOL_WINDOWS]
    return jnp.asarray(np.concatenate([tri3, e64, e128] + band, axis=1), BF16)


def _prep_mixer_params(mix_norm, w_in, conv_w, conv_b, dt_bias, a_log, d_skip, ssd_norm, pool_w, pool_scale,
                       w_out_mix):
    depth = w_in.shape[0]
    xbc_end = SSD_INNER + CONV_DIM
    dt_end = xbc_end + SSD_HEADS
    w_dt = w_in[:, :, xbc_end:dt_end]
    n_pad = DT_LANES - DT_COPIES * SSD_HEADS
    w_dt3 = jnp.concatenate([w_dt] * DT_COPIES + [jnp.zeros((depth, D_MODEL, n_pad), F32)], axis=2)
    win = jnp.concatenate([w_in[:, :, :xbc_end].astype(BF16), w_in[:, :, dt_end:].astype(BF16),
                           w_dt3.astype(BF16)], axis=2)
    pad = jnp.zeros((depth, n_pad), F32)

    def row(a):
        return jnp.pad(a, ((0, 0), (0, CONV_DIM - a.shape[1])))[:, None, :]

    rowp = jnp.concatenate([
        row(mix_norm), conv_w, row(conv_b),
        row(jnp.concatenate([dt_bias] * DT_COPIES + [pad], axis=1)),
        row(jnp.concatenate([a_log] * DT_COPIES + [pad], axis=1)),
        row(jnp.repeat(d_skip, SSD_HEAD_DIM, axis=1)), row(ssd_norm), row(pool_scale),
        jnp.zeros((depth, ROW_PARAM_ROWS - ROW_POOL_SCALE - 1, CONV_DIM), F32)], axis=1)
    return rowp, win, w_out_mix.astype(BF16), pool_w.astype(BF16)


def _mixer(h, layer, params):
    bsz, s, _ = h.shape
    ts = SEQ_BLOCK
    nch = ts // CHUNK
    consts = _mixer_constants()
    in_specs = [pl.BlockSpec((1, ts, D_MODEL), lambda b, j: (b, j, 0))]
    in_specs += [_layer_spec(a.shape, layer) for a in params]
    in_specs += [_const_spec(consts.shape)]
    return pl.pallas_call(
        _mixer_kernel,
        grid=(bsz, s // ts),
        in_specs=in_specs,
        out_specs=pl.BlockSpec((1, ts, D_MODEL), lambda b, j: (b, j, 0)),
        out_shape=jax.ShapeDtypeStruct(h.shape, F32),
        scratch_shapes=[
            pltpu.VMEM((ts, SSD_INNER), F32),
            pltpu.VMEM((CONV_HALO, CONV_DIM), F32),
            pltpu.VMEM((ts, CONV_DIM), F32),
            pltpu.VMEM((ts, POOL_WIDTH), F32),
            pltpu.VMEM((CHUNK + ts, POOL_WIDTH), BF16),
            pltpu.VMEM((SSD_GROUPS, SSD_STATE, GROUP_WIDTH), F32),
            pltpu.VMEM((nch, CHUNK, CHUNK), F32),
            pltpu.VMEM((ts, DT_LANES), BF16),
            pltpu.VMEM((ts, SSD_INNER), F32),
            pltpu.VMEM((ts, SSD_INNER), BF16),
            pltpu.VMEM((ts, SSD_INNER), BF16),
            pltpu.VMEM((ts, SSD_INNER), BF16),
            pltpu.VMEM((nch, SSD_GROUPS, SSD_STATE, CHUNK), BF16),
            pltpu.VMEM((nch, SSD_GROUPS, CHUNK, SSD_STATE), BF16),
            pltpu.VMEM((nch, CHUNK, CHUNK), F32),
            pltpu.VMEM((ts, SSD_INNER), F32),
            pltpu.VMEM((ts, SSD_INNER), F32),
            pltpu.VMEM((ts, D_MIX), BF16),
            pltpu.VMEM((ts, D_MODEL), F32),
        ],
        compiler_params=pltpu.CompilerParams(
            dimension_semantics=("parallel", "arbitrary"), vmem_limit_bytes=VMEM_LIMIT_BYTES),
        name="mixer",
    )(h, *params, consts)


def _kv_kernel(mem_ref, g_ref, wkv_ref, kt_ref, v_ref):
    mn = _rms(mem_ref[0], g_ref[...]).astype(BF16)
    kv = _dot(mn, wkv_ref[...])
    kt_ref[0] = kv[:, 0:D_MODEL].T.astype(BF16)
    v_ref[0] = kv[:, D_MODEL:2 * D_MODEL].astype(BF16)


def _kv(mem, layer, g, w_kv):
    bsz = mem.shape[0]
    return pl.pallas_call(
        _kv_kernel,
        grid=(bsz,),
        in_specs=[pl.BlockSpec((1, MEM_LEN, D_MODEL), lambda b: (b, 0, 0)),
                  _layer_spec(g.shape, layer), _layer_spec(w_kv.shape, layer)],
        out_specs=[pl.BlockSpec((1, D_MODEL, MEM_LEN), lambda b: (b, 0, 0)),
                   pl.BlockSpec((1, MEM_LEN, D_MODEL), lambda b: (b, 0, 0))],
        out_shape=[jax.ShapeDtypeStruct((bsz, D_MODEL, MEM_LEN), BF16),
                   jax.ShapeDtypeStruct((bsz, MEM_LEN, D_MODEL), BF16)],
        compiler_params=pltpu.CompilerParams(
            dimension_semantics=("parallel",), vmem_limit_bytes=VMEM_LIMIT_BYTES),
        name="memory_kv",
    )(mem, g, w_kv)


def _xattn_ffn_kernel(h_ref, gx_ref, wq_ref, kt_ref, v_ref, wo_ref, gf_ref, wgu_ref, wd_ref, gfin_ref, out_ref,
                      o_s, a_s, *, final_norm):
    x = h_ref[0]
    u = _rms(x, gx_ref[...]).astype(BF16)
    q = (_dot(u, wq_ref[...]) * (XATTN_HEAD_DIM ** -0.5)).astype(BF16)
    head_cols = [slice(hd * XATTN_HEAD_DIM, (hd + 1) * XATTN_HEAD_DIM) for hd in range(XATTN_HEADS)]
    scores = [_dot(q[:, cols], kt_ref[0, cols, :]) for cols in head_cols]
    for cols, s in zip(head_cols, scores):
        p = jnp.exp(s - jnp.max(s, axis=-1, keepdims=True))
        den = jnp.sum(p, axis=-1, keepdims=True)
        o = _dot(p.astype(BF16), v_ref[0, :, cols])
        o_s[:, cols] = (o * (1.0 / den)).astype(BF16)
    x = x + _dot(o_s[...], wo_ref[...])

    u = _rms(x, gf_ref[...]).astype(BF16)
    for ci in range(D_FF // FF_COL_CHUNK):
        lo = ci * FF_COL_CHUNK
        gate = _dot(u, wgu_ref[:, lo:lo + FF_COL_CHUNK])
        up = _dot(u, wgu_ref[:, D_FF + lo:D_FF + lo + FF_COL_CHUNK])
        a_s[:, lo:lo + FF_COL_CHUNK] = (_silu(gate) * up).astype(BF16)
    y = x + _dot(a_s[...], wd_ref[...])
    if final_norm:
        y = _rms(y, gfin_ref[...])
    out_ref[0] = y


def _xattn_ffn(h, layer, gx, w_q, kt, v, w_o, gf, w_gate_up, w_down, g_final, final_norm):
    bsz, s, _ = h.shape
    tq = SEQ_BLOCK
    return pl.pallas_call(
        functools.partial(_xattn_ffn_kernel, final_norm=final_norm),
        grid=(bsz, s // tq),
        in_specs=[pl.BlockSpec((1, tq, D_MODEL), lambda b, j: (b, j, 0)),
                  _layer_spec(gx.shape, layer), _layer_spec(w_q.shape, layer),
                  pl.BlockSpec((1, D_MODEL, MEM_LEN), lambda b, j: (b, 0, 0)),
                  pl.BlockSpec((1, MEM_LEN, D_MODEL), lambda b, j: (b, 0, 0)),
                  _layer_spec(w_o.shape, layer), _layer_spec(gf.shape, layer),
                  _layer_spec(w_gate_up.shape, layer), _layer_spec(w_down.shape, layer),
                  _const_spec((1, D_MODEL))],
        out_specs=pl.BlockSpec((1, tq, D_MODEL), lambda b, j: (b, j, 0)),
        out_shape=jax.ShapeDtypeStruct(h.shape, F32),
        scratch_shapes=[pltpu.VMEM((tq, D_MODEL), BF16), pltpu.VMEM((tq, D_FF), BF16)],
        compiler_params=pltpu.CompilerParams(
            dimension_semantics=("parallel", "parallel"), vmem_limit_bytes=VMEM_LIMIT_BYTES),
        name="xattn_swiglu_final" if final_norm else "xattn_swiglu",
    )(h, gx, w_q, kt, v, w_o, gf, w_gate_up, w_down, g_final.reshape(1, -1))


def kernel(x, mem, mix_norm, w_in, conv_w, conv_b, dt_bias, a_log, d_skip, ssd_norm, pool_w, pool_scale,
           w_out_mix, xattn_norm, mem_norm, w_q, w_kv, w_o, ffn_norm, w_gate_up, w_down, final_norm):
    depth = w_in.shape[0]
    row = lambda a: a.reshape(depth, 1, -1)
    mixer_params = _prep_mixer_params(mix_norm, w_in, conv_w, conv_b, dt_bias, a_log, d_skip, ssd_norm, pool_w,
                                      pool_scale, w_out_mix)
    xattn_g, mem_g, ffn_g = row(xattn_norm), row(mem_norm), row(ffn_norm)
    wq_b, wkv_b, wo_b = w_q.astype(BF16), w_kv.astype(BF16), w_o.astype(BF16)
    wgu_b, wd_b = w_gate_up.astype(BF16), w_down.astype(BF16)
    h = x
    for l in range(depth):
        h = _mixer(h, l, mixer_params)
        kt, v = _kv(mem, l, mem_g, wkv_b)
        h = _xattn_ffn(h, l, xattn_g, wq_b, kt, v, wo_b, ffn_g, wgu_b, wd_b, final_norm,
                       final_norm=(l == depth - 1))
    return h
```

```python
import functools

import numpy as np
import jax
import jax.numpy as jnp
from jax import lax
from jax.experimental import pallas as pl
from jax.experimental.pallas import tpu as pltpu

F32 = jnp.float32
BF16 = jnp.bfloat16

D_MODEL = 1024
SSD_INNER = 1024
SSD_HEADS = 16
SSD_HEAD_DIM = 64
SSD_GROUPS = 2
SSD_STATE = 128
GROUP_WIDTH = SSD_INNER // SSD_GROUPS
CONV_WIDTH = 4
CONV_DIM = SSD_INNER + 2 * SSD_GROUPS * SSD_STATE
CHUNK = 128
POOL_WINDOWS = (2, 4, 8, 16)
POOL_WIDTH = 1024
POOL_GROUP_DIM = 256
D_MIX = 2048
DT_LANES = 128
DT_COPIES = 3
IN_COLS = SSD_INNER + CONV_DIM + POOL_WIDTH + DT_LANES
MEM_LEN = 256
XATTN_HEADS = 4
XATTN_HEAD_DIM = 256
D_FF = 2816
FF_COL_CHUNK = 1408
EPS = 1e-6
LOG2_E = 1.4426950408889634

ROW_MIX_NORM, ROW_CONV_W, ROW_CONV_B, ROW_DT_BIAS, ROW_A_LOG, ROW_D_SKIP, ROW_SSD_NORM, ROW_POOL_SCALE = (
    0, 1, 5, 6, 7, 8, 9, 10)
ROW_PARAM_ROWS = 16
CST_TRI = 0
CST_E64 = CST_TRI + DT_COPIES * CHUNK
CST_E128 = CST_E64 + SSD_INNER
CST_BAND = CST_E128 + SSD_HEADS * CHUNK

CONV_HALO = 8
CONV_COLS = 256
SEQ_BLOCK = 512
W_IN_ROW_BLOCK = 256
VMEM_LIMIT_BYTES = 56 * 1024 * 1024


def _rms(x, g):
    ms = jnp.mean(x * x, axis=-1, keepdims=True)
    return x * lax.rsqrt(ms + EPS) * g


def _silu(x):
    h = 0.5 * x
    return h + h * jnp.tanh(h)


def _softplus(x):
    return jnp.maximum(x, 0.0) + jnp.log1p(jnp.exp(-jnp.abs(x)))


def _dot(a, b):
    return jnp.dot(a, b, preferred_element_type=F32)


def _split3_f32(x):
    hi = x.astype(BF16).astype(F32)
    r1 = x - hi
    mid = r1.astype(BF16).astype(F32)
    return hi, mid, r1 - mid


def _pack3(x, lane):
    hi, mid, lo = _split3_f32(x)
    return jnp.where(lane < SSD_HEADS, hi, jnp.where(lane < 2 * SSD_HEADS, mid, lo)).astype(BF16)


def _mixer_kernel(h_ref, rowp_ref, win_ref, wout_ref, poolw_ref, cst_ref,
                  out_ref,
                  z_s, halo_s, xbcs_s, vf_s, vb_s, state_s, acst_s, pk_s, eacs_s, xdec_s, xlo_s, xhi_s,
                  bmt_s, cmb_s, cb_s, y_s, yoff_s, ycat_s, acc_s):
    ts = z_s.shape[0]
    nch = ts // CHUNK
    j = pl.program_id(1)

    @pl.when(j == 0)
    def _():
        state_s[...] = jnp.zeros_like(state_s)
        halo_s[...] = jnp.zeros_like(halo_s)
        vb_s[0:CHUNK, :] = jnp.zeros((CHUNK, POOL_WIDTH), BF16)

    def rows(c):
        return slice(c * CHUNK, (c + 1) * CHUNK)

    g_ref = rowp_ref.at[ROW_MIX_NORM:ROW_MIX_NORM + 1, 0:D_MODEL]
    convw_ref = rowp_ref.at[ROW_CONV_W:ROW_CONV_W + CONV_WIDTH, :]
    convb_ref = rowp_ref.at[ROW_CONV_B:ROW_CONV_B + 1, :]
    dtb_ref = rowp_ref.at[ROW_DT_BIAS:ROW_DT_BIAS + 1, 0:DT_LANES]
    alog_ref = rowp_ref.at[ROW_A_LOG:ROW_A_LOG + 1, 0:DT_LANES]
    dskip_ref = rowp_ref.at[ROW_D_SKIP:ROW_D_SKIP + 1, 0:SSD_INNER]
    ssdn_ref = rowp_ref.at[ROW_SSD_NORM:ROW_SSD_NORM + 1, 0:SSD_INNER]
    pscale_ref = rowp_ref.at[ROW_POOL_SCALE:ROW_POOL_SCALE + 1, 0:POOL_WIDTH]
    tri_ref = cst_ref.at[:, CST_TRI:CST_TRI + DT_COPIES * CHUNK]
    e64_ref = cst_ref.at[:, CST_E64:CST_E64 + SSD_INNER]
    e128_ref = cst_ref.at[:, CST_E128:CST_E128 + SSD_HEADS * CHUNK]

    def band(gi):
        return cst_ref[:, CST_BAND + gi * 2 * CHUNK:CST_BAND + (gi + 1) * 2 * CHUNK]

    x = h_ref[0]
    u = _rms(x, g_ref[...]).astype(BF16)

    v_off = SSD_INNER + CONV_DIM

    def z_piece(i):
        z_s[:, i * CONV_COLS:(i + 1) * CONV_COLS] = _dot(u, win_ref[:, i * CONV_COLS:(i + 1) * CONV_COLS])

    def v_piece(i):
        v = _dot(u, win_ref[:, v_off + i * CONV_COLS:v_off + (i + 1) * CONV_COLS])
        vf_s[:, i * CONV_COLS:(i + 1) * CONV_COLS] = v
        vb_s[CHUNK:CHUNK + ts, i * CONV_COLS:(i + 1) * CONV_COLS] = v.astype(BF16)

    def conv_group(cg):
        cols = slice(cg * CONV_COLS, (cg + 1) * CONV_COLS)
        res = _dot(u, win_ref[:, SSD_INNER + cg * CONV_COLS:SSD_INNER + (cg + 1) * CONV_COLS])
        w = [convw_ref[k:k + 1, cols] for k in range(CONV_WIDTH)]
        ext = jnp.concatenate([halo_s[:, cols], res], axis=0)
        prev = pltpu.roll(ext, 1, axis=0)
        b = w[1] * ext + w[0] * prev
        a = convb_ref[:, cols] + w[3] * res + w[2] * prev[CONV_HALO:, :]
        xbcs_s[:, cols] = _silu(a + pltpu.roll(b, 2, axis=0)[CONV_HALO:, :])
        halo_s[:, cols] = res[ts - CONV_HALO:ts, :]

    lane = lax.broadcasted_iota(jnp.int32, (CHUNK, CHUNK), 1)
    row = lax.broadcasted_iota(jnp.int32, (CHUNK, CHUNK), 0)
    causal = row >= lane
    lane_b = lax.broadcasted_iota(jnp.int32, (ts, DT_LANES), 1)
    a_row = jnp.where(lane[0:1, :] < DT_COPIES * SSD_HEADS, -jnp.exp(alog_ref[...]), 0.0)

    tpos0 = j * ts + 1 + lax.broadcasted_iota(jnp.int32, (CHUNK, POOL_GROUP_DIM), 0)

    def pool_group(gi):
        w = POOL_WINDOWS[gi]
        cols = slice(gi * POOL_GROUP_DIM, (gi + 1) * POOL_GROUP_DIM)
        ds = []
        for c in range(nch):
            win = _dot(band(gi), vb_s[c * CHUNK:(c + 2) * CHUNK, cols])
            if c == 0:
                mean = win / jnp.minimum(tpos0.astype(F32), float(w))
            else:
                mean = win * (1.0 / w)
            ds.append((mean - vf_s[rows(c), cols]).astype(BF16))
        yp = _dot(jnp.concatenate(ds, axis=0), poolw_ref[gi]) * pscale_ref[:, cols]
        ycat_s[:, SSD_INNER + gi * POOL_GROUP_DIM:SSD_INNER + (gi + 1) * POOL_GROUP_DIM] = yp.astype(BF16)

    def pool_out_piece(k):
        cols = slice(k * CONV_COLS, (k + 1) * CONV_COLS)
        acc_s[:, cols] = x[:, cols] + _dot(ycat_s[:, SSD_INNER:D_MIX], wout_ref[SSD_INNER:D_MIX, cols])

    n_side = SSD_INNER // CONV_COLS
    fillers = ([functools.partial(v_piece, k) for k in range(n_side)]
               + [functools.partial(pool_group, gi) for gi in range(len(POOL_WINDOWS))]
               + [functools.partial(pool_out_piece, k) for k in range(n_side)]
               + [functools.partial(z_piece, k) for k in range(n_side)])

    def fill(n):
        for _ in range(n):
            if fillers:
                fillers.pop(0)()

    dt = _softplus(_dot(u, win_ref[:, IN_COLS - DT_LANES:IN_COLS]) + dtb_ref[...])
    adt_parts = _split3_f32(dt * a_row)
    fill(2)
    conv_group(0)
    acs_c = []
    for c in range(nch):
        stacked = jnp.concatenate([p[rows(c)] for p in adt_parts], axis=0).astype(BF16)
        acs_c.append(_dot(tri_ref[...], stacked) * LOG2_E)
    fill(2)
    conv_group(1)
    for c in range(nch):
        acst_s[c] = acs_c[c].T
    pk_acs = _pack3(jnp.concatenate(acs_c, axis=0), lane_b)
    pk_s[...] = pk_acs
    pk_dt = _pack3(dt, lane_b)
    vb_tail = vb_s[ts:ts + CHUNK, :]
    fill(1)
    conv_group(2)
    acs_x = _dot(pk_acs, e64_ref[...])
    dt_x = _dot(pk_dt, e64_ref[...])
    fill(1)
    conv_group(3)
    eacs_s[...] = jnp.exp2(acs_x)
    fill(2)
    lane_x = lax.broadcasted_iota(jnp.int32, (CHUNK, SSD_INNER), 1) % (2 * SSD_HEAD_DIM)
    for c in range(nch):
        r = rows(c)
        xc = xbcs_s[r, 0:SSD_INNER] * dt_x[r]
        alast = acs_x[(c + 1) * CHUNK - 1:(c + 1) * CHUNK, :]
        xdec_s[r, :] = (xc * jnp.exp2(alast - acs_x[r])).astype(BF16)
        xlo_s[r, :] = jnp.where(lane_x < SSD_HEAD_DIM, xc, 0.0).astype(BF16)
        xhi_s[r, :] = jnp.where(lane_x >= SSD_HEAD_DIM, xc, 0.0).astype(BF16)
        fill(1)
        if c == 1:
            conv_group(4)
    conv_group(5)
    fill(1)
    for c in range(nch):
        for g in range(SSD_GROUPS):
            b_cols = slice(SSD_INNER + g * SSD_STATE, SSD_INNER + (g + 1) * SSD_STATE)
            c_cols = slice(SSD_INNER + (SSD_GROUPS + g) * SSD_STATE, SSD_INNER + (SSD_GROUPS + g + 1) * SSD_STATE)
            bmt_s[c, g] = xbcs_s[rows(c), b_cols].T.astype(BF16)
            cmb_s[c, g] = xbcs_s[rows(c), c_cols].astype(BF16)

    pairs_per_group = SSD_HEADS // SSD_GROUPS // 2
    for p in range(SSD_HEADS // 2):
        g = p // pairs_per_group
        pcols = slice(p * 2 * SSD_HEAD_DIM, (p + 1) * 2 * SSD_HEAD_DIM)
        if p % pairs_per_group == 0:
            for c in range(nch):
                cb_s[c] = _dot(cmb_s[c, g], bmt_s[c, g])
        cpair = _dot(pk_s[...], e128_ref[:, p * 2 * CHUNK:(p + 1) * 2 * CHUNK])
        for c in range(nch):
            ms = []
            for q in range(2):
                hh = 2 * p + q
                dmat = cpair[rows(c), q * CHUNK:(q + 1) * CHUNK] - acst_s[c, hh:hh + 1, :]
                seg = jnp.exp2(jnp.where(causal, dmat, -jnp.inf))
                ms.append((cb_s[c] * seg).astype(BF16))
            rhs = jnp.concatenate([xlo_s[rows(c), pcols], xhi_s[rows(c), pcols]], axis=0)
            y_s[rows(c), pcols] = _dot(jnp.concatenate(ms, axis=1), rhs)
        if p < 1:
            fill(1)
    vb_s[0:CHUNK, :] = vb_tail

    new_st = {}
    for c in range(nch):
        for g in range(SSD_GROUPS):
            gcols = slice(g * GROUP_WIDTH, (g + 1) * GROUP_WIDTH)
            new_st[c, g] = _dot(bmt_s[c, g], xdec_s[rows(c), gcols])
    fill(len(fillers))
    for c in range(nch):
        for g in range(SSD_GROUPS):
            gcols = slice(g * GROUP_WIDTH, (g + 1) * GROUP_WIDTH)
            st = state_s[g]
            yoff_s[rows(c), gcols] = _dot(cmb_s[c, g], st.astype(BF16))
            state_s[g] = st * eacs_s[(c + 1) * CHUNK - 1:(c + 1) * CHUNK, gcols] + new_st[c, g]

    for c in range(nch):
        r = rows(c)
        y = y_s[r, :] + yoff_s[r, :] * eacs_s[r, :] + dskip_ref[...] * xbcs_s[r, 0:SSD_INNER]
        y = y * _silu(z_s[r, :])
        ycat_s[r, 0:SSD_INNER] = _rms(y, ssdn_ref[...]).astype(BF16)
        out_ref[0, r, :] = acc_s[r, :] + _dot(ycat_s[r, 0:SSD_INNER], wout_ref[0:SSD_INNER, :])


def _const_spec(shape):
    nd = len(shape)
    return pl.BlockSpec(shape, lambda *_: (0,) * nd, pipeline_mode=pl.Buffered(1))


def _layer_spec(shape, layer):
    nd = len(shape)
    return pl.BlockSpec((None,) + tuple(shape[1:]), lambda *_: (layer,) + (0,) * (nd - 1),
                        pipeline_mode=pl.Buffered(1))


def _mixer_constants():
    t = np.arange(CHUNK)
    tri = (t[:, None] >= t[None, :]).astype(np.float32)
    tri3 = np.concatenate([tri] * DT_COPIES, axis=1)
    e64 = np.zeros((DT_LANES, SSD_INNER), np.float32)
    e128 = np.zeros((DT_LANES, SSD_HEADS * CHUNK), np.float32)
    for c in range(DT_COPIES):
        for h in range(SSD_HEADS):
            e64[c * SSD_HEADS + h, h * SSD_HEAD_DIM:(h + 1) * SSD_HEAD_DIM] = 1.0
            e128[c * SSD_HEADS + h, h * CHUNK:(h + 1) * CHUNK] = 1.0
    k = np.arange(2 * CHUNK) - CHUNK
    band = [((k[None, :] <= t[:, None]) & (k[None, :] >= t[:, None] - w + 1)).astype(np.float32)
            for w in POOL_WINDOWS]
    return jnp.asarray(np.concatenate([tri3, e64, e128] + band, axis=1), BF16)


def _w_in_relayout_kernel(w_ref, o_ref):
    w = w_ref[0]
    xbc_end = SSD_INNER + CONV_DIM
    dt_end = xbc_end + SSD_HEADS
    o_ref[0, :, 0:xbc_end] = w[:, 0:xbc_end].astype(BF16)
    o_ref[0, :, xbc_end:xbc_end + POOL_WIDTH] = w[:, dt_end:dt_end + POOL_WIDTH].astype(BF16)
    dt = w[:, xbc_end:dt_end]
    zeros = jnp.zeros((w.shape[0], DT_LANES - DT_COPIES * SSD_HEADS), F32)
    o_ref[0, :, IN_COLS - DT_LANES:IN_COLS] = jnp.concatenate([dt] * DT_COPIES + [zeros], axis=1).astype(BF16)


def _relayout_w_in(w_in):
    depth, k, n = w_in.shape
    rb = W_IN_ROW_BLOCK
    return pl.pallas_call(
        _w_in_relayout_kernel,
        grid=(depth, k // rb),
        in_specs=[pl.BlockSpec((1, rb, n), lambda l, r: (l, r, 0))],
        out_specs=pl.BlockSpec((1, rb, IN_COLS), lambda l, r: (l, r, 0)),
        out_shape=jax.ShapeDtypeStruct((depth, k, IN_COLS), BF16),
        compiler_params=pltpu.CompilerParams(
            dimension_semantics=("parallel", "parallel"), vmem_limit_bytes=VMEM_LIMIT_BYTES),
        name="w_in_relayout",
    )(w_in)


def _prep_mixer_params(mix_norm, w_in, conv_w, conv_b, dt_bias, a_log, d_skip, ssd_norm, pool_w, pool_scale,
                       w_out_mix):
    depth = w_in.shape[0]
    n_pad = DT_LANES - DT_COPIES * SSD_HEADS
    win = _relayout_w_in(w_in)
    pad = jnp.zeros((depth, n_pad), F32)

    def row(a):
        return jnp.pad(a, ((0, 0), (0, CONV_DIM - a.shape[1])))[:, None, :]

    rowp = jnp.concatenate([
        row(mix_norm), conv_w, row(conv_b),
        row(jnp.concatenate([dt_bias] * DT_COPIES + [pad], axis=1)),
        row(jnp.concatenate([a_log] * DT_COPIES + [pad], axis=1)),
        row(jnp.repeat(d_skip, SSD_HEAD_DIM, axis=1)), row(ssd_norm), row(pool_scale),
        jnp.zeros((depth, ROW_PARAM_ROWS - ROW_POOL_SCALE - 1, CONV_DIM), F32)], axis=1)
    return rowp, win, w_out_mix.astype(BF16), pool_w.astype(BF16)


def _mixer(h, layer, params):
    bsz, s, _ = h.shape
    ts = SEQ_BLOCK
    nch = ts // CHUNK
    consts = _mixer_constants()
    in_specs = [pl.BlockSpec((1, ts, D_MODEL), lambda b, j: (b, j, 0))]
    in_specs += [_layer_spec(a.shape, layer) for a in params]
    in_specs += [_const_spec(consts.shape)]
    return pl.pallas_call(
        _mixer_kernel,
        grid=(bsz, s // ts),
        in_specs=in_specs,
        out_specs=pl.BlockSpec((1, ts, D_MODEL), lambda b, j: (b, j, 0)),
        out_shape=jax.ShapeDtypeStruct(h.shape, F32),
        scratch_shapes=[
            pltpu.VMEM((ts, SSD_INNER), F32),
            pltpu.VMEM((CONV_HALO, CONV_DIM), F32),
            pltpu.VMEM((ts, CONV_DIM), F32),
            pltpu.VMEM((ts, POOL_WIDTH), F32),
            pltpu.VMEM((CHUNK + ts, POOL_WIDTH), BF16),
            pltpu.VMEM((SSD_GROUPS, SSD_STATE, GROUP_WIDTH), F32),
            pltpu.VMEM((nch, CHUNK, CHUNK), F32),
            pltpu.VMEM((ts, DT_LANES), BF16),
            pltpu.VMEM((ts, SSD_INNER), F32),
            pltpu.VMEM((ts, SSD_INNER), BF16),
            pltpu.VMEM((ts, SSD_INNER), BF16),
            pltpu.VMEM((ts, SSD_INNER), BF16),
            pltpu.VMEM((nch, SSD_GROUPS, SSD_STATE, CHUNK), BF16),
            pltpu.VMEM((nch, SSD_GROUPS, CHUNK, SSD_STATE), BF16),
            pltpu.VMEM((nch, CHUNK, CHUNK), F32),
            pltpu.VMEM((ts, SSD_INNER), F32),
            pltpu.VMEM((ts, SSD_INNER), F32),
            pltpu.VMEM((ts, D_MIX), BF16),
            pltpu.VMEM((ts, D_MODEL), F32),
        ],
        compiler_params=pltpu.CompilerParams(
            dimension_semantics=("parallel", "arbitrary"), vmem_limit_bytes=VMEM_LIMIT_BYTES),
        name="mixer",
    )(h, *params, consts)


def _kv_kernel(mem_ref, g_ref, wkv_ref, kt_ref, v_ref):
    mn = _rms(mem_ref[0], g_ref[...]).astype(BF16)
    kv = _dot(mn, wkv_ref[...])
    kt_ref[0] = kv[:, 0:D_MODEL].T.astype(BF16)
    v_ref[0] = kv[:, D_MODEL:2 * D_MODEL].astype(BF16)


def _kv(mem, layer, g, w_kv):
    bsz = mem.shape[0]
    return pl.pallas_call(
        _kv_kernel,
        grid=(bsz,),
        in_specs=[pl.BlockSpec((1, MEM_LEN, D_MODEL), lambda b: (b, 0, 0)),
                  _layer_spec(g.shape, layer), _layer_spec(w_kv.shape, layer)],
        out_specs=[pl.BlockSpec((1, D_MODEL, MEM_LEN), lambda b: (b, 0, 0)),
                   pl.BlockSpec((1, MEM_LEN, D_MODEL), lambda b: (b, 0, 0))],
        out_shape=[jax.ShapeDtypeStruct((bsz, D_MODEL, MEM_LEN), BF16),
                   jax.ShapeDtypeStruct((bsz, MEM_LEN, D_MODEL), BF16)],
        compiler_params=pltpu.CompilerParams(
            dimension_semantics=("parallel",), vmem_limit_bytes=VMEM_LIMIT_BYTES),
        name="memory_kv",
    )(mem, g, w_kv)


def _xattn_ffn_kernel(h_ref, gx_ref, wq_ref, kt_ref, v_ref, wo_ref, gf_ref, wgu_ref, wd_ref, gfin_ref, out_ref,
                      o_s, a_s, h2_s, u2_s, *, final_norm):
    i = pl.program_id(0)

    @pl.when(i == 0)
    def _():
        h2_s[...] = jnp.zeros_like(h2_s)
        u2_s[...] = jnp.zeros_like(u2_s)

    head_cols = [slice(hd * XATTN_HEAD_DIM, (hd + 1) * XATTN_HEAD_DIM) for hd in range(XATTN_HEADS)]
    n_ff = D_FF // FF_COL_CHUNK
    x = h_ref[0]
    xp = h2_s[...]
    up_ = u2_s[...]

    def ffn_dots(ci):
        lo = ci * FF_COL_CHUNK
        return (_dot(up_, wgu_ref[:, lo:lo + FF_COL_CHUNK]),
                _dot(up_, wgu_ref[:, D_FF + lo:D_FF + lo + FF_COL_CHUNK]))

    def ffn_act(ci, gate_up):
        lo = ci * FF_COL_CHUNK
        a_s[:, lo:lo + FF_COL_CHUNK] = (_silu(gate_up[0]) * gate_up[1]).astype(BF16)

    gu = ffn_dots(0)
    u = _rms(x, gx_ref[...]).astype(BF16)
    q = (_dot(u, wq_ref[...]) * (XATTN_HEAD_DIM ** -0.5)).astype(BF16)
    ffn_act(0, gu)
    scores = [_dot(q[:, cols], kt_ref[0, cols, :]) for cols in head_cols]
    for ci in range(1, n_ff):
        gu = ffn_dots(ci)
        if ci == 1:
            ps = []
            for s in scores:
                p = jnp.exp(s - jnp.max(s, axis=-1, keepdims=True))
                ps.append((p.astype(BF16), jnp.sum(p, axis=-1, keepdims=True)))
        ffn_act(ci, gu)
    for cols, (p, den) in zip(head_cols, ps):
        o = _dot(p, v_ref[0, :, cols])
        o_s[:, cols] = (o * (1.0 / den)).astype(BF16)
    x2 = x + _dot(o_s[...], wo_ref[...])
    y = xp + _dot(a_s[...], wd_ref[...])
    h2_s[...] = x2
    u2_s[...] = _rms(x2, gf_ref[...]).astype(BF16)
    if final_norm:
        y = _rms(y, gfin_ref[...])
    out_ref[0] = y


def _xattn_ffn(h, layer, gx, w_q, kt, v, w_o, gf, w_gate_up, w_down, g_final, final_norm):
    bsz, s, _ = h.shape
    tq = SEQ_BLOCK
    nps = s // tq
    n_blocks = bsz * nps

    def cur(i):
        blk = jnp.minimum(i, n_blocks - 1)
        return (blk // nps, blk % nps, 0)

    def prev(i):
        blk = jnp.maximum(i - 1, 0)
        return (blk // nps, blk % nps, 0)

    def mem_of_cur(i):
        return (jnp.minimum(i, n_blocks - 1) // nps, 0, 0)

    return pl.pallas_call(
        functools.partial(_xattn_ffn_kernel, final_norm=final_norm),
        grid=(n_blocks + 1,),
        in_specs=[pl.BlockSpec((1, tq, D_MODEL), cur),
                  _layer_spec(gx.shape, layer), _layer_spec(w_q.shape, layer),
                  pl.BlockSpec((1, D_MODEL, MEM_LEN), mem_of_cur),
                  pl.BlockSpec((1, MEM_LEN, D_MODEL), mem_of_cur),
                  _layer_spec(w_o.shape, layer), _layer_spec(gf.shape, layer),
                  _layer_spec(w_gate_up.shape, layer), _layer_spec(w_down.shape, layer),
                  _const_spec((1, D_MODEL))],
        out_specs=pl.BlockSpec((1, tq, D_MODEL), prev),
        out_shape=jax.ShapeDtypeStruct(h.shape, F32),
        scratch_shapes=[pltpu.VMEM((tq, D_MODEL), BF16), pltpu.VMEM((tq, D_FF), BF16),
                        pltpu.VMEM((tq, D_MODEL), F32), pltpu.VMEM((tq, D_MODEL), BF16)],
        compiler_params=pltpu.CompilerParams(
            dimension_semantics=("arbitrary",), vmem_limit_bytes=VMEM_LIMIT_BYTES),
        name="xattn_swiglu_final" if final_norm else "xattn_swiglu",
    )(h, gx, w_q, kt, v, w_o, gf, w_gate_up, w_down, g_final.reshape(1, -1))


def kernel(x, mem, mix_norm, w_in, conv_w, conv_b, dt_bias, a_log, d_skip, ssd_norm, pool_w, pool_scale,
           w_out_mix, xattn_norm, mem_norm, w_q, w_kv, w_o, ffn_norm, w_gate_up, w_down, final_norm):
    depth = w_in.shape[0]
    row = lambda a: a.reshape(depth, 1, -1)
    mixer_params = _prep_mixer_params(mix_norm, w_in, conv_w, conv_b, dt_bias, a_log, d_skip, ssd_norm, pool_w,
                                      pool_scale, w_out_mix)
    xattn_g, mem_g, ffn_g = row(xattn_norm), row(mem_norm), row(ffn_norm)
    wq_b, wkv_b, wo_b = w_q.astype(BF16), w_kv.astype(BF16), w_o.astype(BF16)
    wgu_b, wd_b = w_gate_up.astype(BF16), w_down.astype(BF16)
    h = x
    for l in range(depth):
        h = _mixer(h, l, mixer_params)
        kt, v = _kv(mem, l, mem_g, wkv_b)
        h = _xattn_ffn(h, l, xattn_g, wq_b, kt, v, wo_b, ffn_g, wgu_b, wd_b, final_norm,
                       final_norm=(l == depth - 1))
    return h
```

```python
import functools

import numpy as np
import jax
import jax.numpy as jnp
from jax import lax
from jax.experimental import pallas as pl
from jax.experimental.pallas import tpu as pltpu

F32 = jnp.float32
BF16 = jnp.bfloat16

D_MODEL = 1024
SSD_INNER = 1024
SSD_HEADS = 16
SSD_HEAD_DIM = 64
SSD_GROUPS = 2
SSD_STATE = 128
GROUP_WIDTH = SSD_INNER // SSD_GROUPS
CONV_WIDTH = 4
CONV_DIM = SSD_INNER + 2 * SSD_GROUPS * SSD_STATE
CHUNK = 128
POOL_WINDOWS = (2, 4, 8, 16)
POOL_WIDTH = 1024
POOL_GROUP_DIM = 256
D_MIX = 2048
DT_LANES = 128
DT_COPIES = 3
IN_COLS = SSD_INNER + CONV_DIM + POOL_WIDTH + DT_LANES
MEM_LEN = 256
XATTN_HEADS = 4
XATTN_HEAD_DIM = 256
D_FF = 2816
FF_COL_CHUNK = 1408
EPS = 1e-6
LOG2_E = 1.4426950408889634

ROW_MIX_NORM, ROW_CONV_W, ROW_CONV_B, ROW_DT_BIAS, ROW_A_LOG, ROW_D_SKIP, ROW_SSD_NORM, ROW_POOL_SCALE = (
    0, 1, 5, 6, 7, 8, 9, 10)
ROW_PARAM_ROWS = 16
CST_TRI = 0
CST_E64 = CST_TRI + DT_COPIES * CHUNK
CST_BAND = CST_E64 + SSD_INNER

CONV_HALO = 8
CONV_COLS = 256
SEQ_BLOCK = 512
W_IN_ROW_BLOCK = 256
VMEM_LIMIT_BYTES = 56 * 1024 * 1024


def _rms(x, g):
    ms = jnp.mean(x * x, axis=-1, keepdims=True)
    return x * lax.rsqrt(ms + EPS) * g


def _silu(x):
    h = 0.5 * x
    return h + h * jnp.tanh(h)


def _softplus(x):
    return jnp.maximum(x, 0.0) + jnp.log1p(jnp.exp(-jnp.abs(x)))


def _dot(a, b):
    return jnp.dot(a, b, preferred_element_type=F32)


def _split3_f32(x):
    hi = x.astype(BF16).astype(F32)
    r1 = x - hi
    mid = r1.astype(BF16).astype(F32)
    return hi, mid, r1 - mid


def _pack3(x, lane):
    hi, mid, lo = _split3_f32(x)
    return jnp.where(lane < SSD_HEADS, hi, jnp.where(lane < 2 * SSD_HEADS, mid, lo)).astype(BF16)


def _mixer_kernel(h_ref, rowp_ref, win_ref, wout_ref, poolw_ref, cst_ref,
                  out_ref,
                  z_s, halo_s, xbcs_s, vf_s, vb_s, state_s, acst_s, eacs_s, xdec_s, xlo_s, xhi_s,
                  bmt_s, cmb_s, cb_s, y_s, yoff_s, ycat_s, acc_s):
    ts = z_s.shape[0]
    nch = ts // CHUNK
    j = pl.program_id(1)

    @pl.when(j == 0)
    def _():
        state_s[...] = jnp.zeros_like(state_s)
        halo_s[...] = jnp.zeros_like(halo_s)
        vb_s[0:CHUNK, :] = jnp.zeros((CHUNK, POOL_WIDTH), BF16)

    def rows(c):
        return slice(c * CHUNK, (c + 1) * CHUNK)

    g_ref = rowp_ref.at[ROW_MIX_NORM:ROW_MIX_NORM + 1, 0:D_MODEL]
    convw_ref = rowp_ref.at[ROW_CONV_W:ROW_CONV_W + CONV_WIDTH, :]
    convb_ref = rowp_ref.at[ROW_CONV_B:ROW_CONV_B + 1, :]
    dtb_ref = rowp_ref.at[ROW_DT_BIAS:ROW_DT_BIAS + 1, 0:DT_LANES]
    alog_ref = rowp_ref.at[ROW_A_LOG:ROW_A_LOG + 1, 0:DT_LANES]
    dskip_ref = rowp_ref.at[ROW_D_SKIP:ROW_D_SKIP + 1, 0:SSD_INNER]
    ssdn_ref = rowp_ref.at[ROW_SSD_NORM:ROW_SSD_NORM + 1, 0:SSD_INNER]
    pscale_ref = rowp_ref.at[ROW_POOL_SCALE:ROW_POOL_SCALE + 1, 0:POOL_WIDTH]
    tri_ref = cst_ref.at[:, CST_TRI:CST_TRI + DT_COPIES * CHUNK]
    e64_ref = cst_ref.at[:, CST_E64:CST_E64 + SSD_INNER]

    def band(gi):
        return cst_ref[:, CST_BAND + gi * 2 * CHUNK:CST_BAND + (gi + 1) * 2 * CHUNK]

    x = h_ref[0]
    u = _rms(x, g_ref[...]).astype(BF16)

    v_off = SSD_INNER + CONV_DIM

    def z_piece(i):
        z_s[:, i * CONV_COLS:(i + 1) * CONV_COLS] = _dot(u, win_ref[:, i * CONV_COLS:(i + 1) * CONV_COLS])

    def v_piece(i):
        v = _dot(u, win_ref[:, v_off + i * CONV_COLS:v_off + (i + 1) * CONV_COLS])
        vf_s[:, i * CONV_COLS:(i + 1) * CONV_COLS] = v
        vb_s[CHUNK:CHUNK + ts, i * CONV_COLS:(i + 1) * CONV_COLS] = v.astype(BF16)

    def conv_group(cg):
        cols = slice(cg * CONV_COLS, (cg + 1) * CONV_COLS)
        res = _dot(u, win_ref[:, SSD_INNER + cg * CONV_COLS:SSD_INNER + (cg + 1) * CONV_COLS])
        w = [convw_ref[k:k + 1, cols] for k in range(CONV_WIDTH)]
        ext = jnp.concatenate([halo_s[:, cols], res], axis=0)
        prev = pltpu.roll(ext, 1, axis=0)
        b = w[1] * ext + w[0] * prev
        a = convb_ref[:, cols] + w[3] * res + w[2] * prev[CONV_HALO:, :]
        xbcs_s[:, cols] = _silu(a + pltpu.roll(b, 2, axis=0)[CONV_HALO:, :])
        halo_s[:, cols] = res[ts - CONV_HALO:ts, :]

    lane = lax.broadcasted_iota(jnp.int32, (CHUNK, CHUNK), 1)
    row = lax.broadcasted_iota(jnp.int32, (CHUNK, CHUNK), 0)
    causal = row >= lane
    lane_b = lax.broadcasted_iota(jnp.int32, (ts, DT_LANES), 1)
    a_row = jnp.where(lane[0:1, :] < DT_COPIES * SSD_HEADS, -jnp.exp(alog_ref[...]), 0.0)

    tpos0 = j * ts + 1 + lax.broadcasted_iota(jnp.int32, (CHUNK, POOL_GROUP_DIM), 0)

    def pool_group(gi):
        w = POOL_WINDOWS[gi]
        cols = slice(gi * POOL_GROUP_DIM, (gi + 1) * POOL_GROUP_DIM)
        ds = []
        for c in range(nch):
            win = _dot(band(gi), vb_s[c * CHUNK:(c + 2) * CHUNK, cols])
            if c == 0:
                mean = win / jnp.minimum(tpos0.astype(F32), float(w))
            else:
                mean = win * (1.0 / w)
            ds.append((mean - vf_s[rows(c), cols]).astype(BF16))
        yp = _dot(jnp.concatenate(ds, axis=0), poolw_ref[gi]) * pscale_ref[:, cols]
        ycat_s[:, SSD_INNER + gi * POOL_GROUP_DIM:SSD_INNER + (gi + 1) * POOL_GROUP_DIM] = yp.astype(BF16)

    def pool_out_piece(k):
        cols = slice(k * CONV_COLS, (k + 1) * CONV_COLS)
        acc_s[:, cols] = x[:, cols] + _dot(ycat_s[:, SSD_INNER:D_MIX], wout_ref[SSD_INNER:D_MIX, cols])

    n_side = SSD_INNER // CONV_COLS
    fillers = ([functools.partial(v_piece, k) for k in range(n_side)]
               + [functools.partial(pool_group, gi) for gi in range(len(POOL_WINDOWS))]
               + [functools.partial(pool_out_piece, k) for k in range(n_side)]
               + [functools.partial(z_piece, k) for k in range(n_side)])

    def fill(n):
        for _ in range(n):
            if fillers:
                fillers.pop(0)()

    dt = _softplus(_dot(u, win_ref[:, IN_COLS - DT_LANES:IN_COLS]) + dtb_ref[...])
    adt_parts = _split3_f32(dt * a_row)
    fill(2)
    conv_group(0)
    acs_c = []
    for c in range(nch):
        stacked = jnp.concatenate([p[rows(c)] for p in adt_parts], axis=0).astype(BF16)
        acs_c.append(_dot(tri_ref[...], stacked) * LOG2_E)
    fill(2)
    conv_group(1)
    for c in range(nch):
        acst_s[c] = acs_c[c].T
    pk_acs = _pack3(jnp.concatenate(acs_c, axis=0), lane_b)
    pk_dt = _pack3(dt, lane_b)
    vb_tail = vb_s[ts:ts + CHUNK, :]
    fill(1)
    conv_group(2)
    acs_x = _dot(pk_acs, e64_ref[...])
    dt_x = _dot(pk_dt, e64_ref[...])
    fill(1)
    conv_group(3)
    eacs_s[...] = jnp.exp2(acs_x)
    fill(2)
    lane_x = lax.broadcasted_iota(jnp.int32, (CHUNK, SSD_INNER), 1) % (2 * SSD_HEAD_DIM)
    for c in range(nch):
        r = rows(c)
        xc = xbcs_s[r, 0:SSD_INNER] * dt_x[r]
        alast = acs_x[(c + 1) * CHUNK - 1:(c + 1) * CHUNK, :]
        xdec_s[r, :] = (xc * jnp.exp2(alast - acs_x[r])).astype(BF16)
        xlo_s[r, :] = jnp.where(lane_x < SSD_HEAD_DIM, xc, 0.0).astype(BF16)
        xhi_s[r, :] = jnp.where(lane_x >= SSD_HEAD_DIM, xc, 0.0).astype(BF16)
        fill(1)
        if c == 1:
            conv_group(4)
    conv_group(5)
    fill(1)
    for c in range(nch):
        for g in range(SSD_GROUPS):
            b_cols = slice(SSD_INNER + g * SSD_STATE, SSD_INNER + (g + 1) * SSD_STATE)
            c_cols = slice(SSD_INNER + (SSD_GROUPS + g) * SSD_STATE, SSD_INNER + (SSD_GROUPS + g + 1) * SSD_STATE)
            bmt_s[c, g] = xbcs_s[rows(c), b_cols].T.astype(BF16)
            cmb_s[c, g] = xbcs_s[rows(c), c_cols].astype(BF16)

    pairs_per_group = SSD_HEADS // SSD_GROUPS // 2
    for p in range(SSD_HEADS // 2):
        g = p // pairs_per_group
        pcols = slice(p * 2 * SSD_HEAD_DIM, (p + 1) * 2 * SSD_HEAD_DIM)
        if p % pairs_per_group == 0:
            for c in range(nch):
                cb_s[c] = _dot(cmb_s[c, g], bmt_s[c, g])
        for c in range(nch):
            ms = []
            for q in range(2):
                hh = 2 * p + q
                dmat = jnp.broadcast_to(acs_c[c][:, hh:hh + 1], (CHUNK, CHUNK)) - acst_s[c, hh:hh + 1, :]
                seg = jnp.exp2(jnp.where(causal, dmat, -jnp.inf))
                ms.append((cb_s[c] * seg).astype(BF16))
            rhs = jnp.concatenate([xlo_s[rows(c), pcols], xhi_s[rows(c), pcols]], axis=0)
            y_s[rows(c), pcols] = _dot(jnp.concatenate(ms, axis=1), rhs)
        if p < 1:
            fill(1)
    vb_s[0:CHUNK, :] = vb_tail

    new_st = {}
    for c in range(nch):
        for g in range(SSD_GROUPS):
            gcols = slice(g * GROUP_WIDTH, (g + 1) * GROUP_WIDTH)
            new_st[c, g] = _dot(bmt_s[c, g], xdec_s[rows(c), gcols])
    fill(len(fillers))
    for c in range(nch):
        for g in range(SSD_GROUPS):
            gcols = slice(g * GROUP_WIDTH, (g + 1) * GROUP_WIDTH)
            st = state_s[g]
            yoff_s[rows(c), gcols] = _dot(cmb_s[c, g], st.astype(BF16))
            state_s[g] = st * eacs_s[(c + 1) * CHUNK - 1:(c + 1) * CHUNK, gcols] + new_st[c, g]

    for c in range(nch):
        r = rows(c)
        y = y_s[r, :] + yoff_s[r, :] * eacs_s[r, :] + dskip_ref[...] * xbcs_s[r, 0:SSD_INNER]
        y = y * _silu(z_s[r, :])
        ycat_s[r, 0:SSD_INNER] = _rms(y, ssdn_ref[...]).astype(BF16)
        out_ref[0, r, :] = acc_s[r, :] + _dot(ycat_s[r, 0:SSD_INNER], wout_ref[0:SSD_INNER, :])


def _const_spec(shape):
    nd = len(shape)
    return pl.BlockSpec(shape, lambda *_: (0,) * nd, pipeline_mode=pl.Buffered(1))


def _layer_spec(shape, layer):
    nd = len(shape)
    return pl.BlockSpec((None,) + tuple(shape[1:]), lambda *_: (layer,) + (0,) * (nd - 1),
                        pipeline_mode=pl.Buffered(1))


def _mixer_constants():
    t = np.arange(CHUNK)
    tri = (t[:, None] >= t[None, :]).astype(np.float32)
    tri3 = np.concatenate([tri] * DT_COPIES, axis=1)
    e64 = np.zeros((DT_LANES, SSD_INNER), np.float32)
    for c in range(DT_COPIES):
        for h in range(SSD_HEADS):
            e64[c * SSD_HEADS + h, h * SSD_HEAD_DIM:(h + 1) * SSD_HEAD_DIM] = 1.0
    k = np.arange(2 * CHUNK) - CHUNK
    band = [((k[None, :] <= t[:, None]) & (k[None, :] >= t[:, None] - w + 1)).astype(np.float32)
            for w in POOL_WINDOWS]
    return jnp.asarray(np.concatenate([tri3, e64] + band, axis=1), BF16)


def _w_in_relayout_kernel(w_ref, o_ref):
    w = w_ref[0]
    xbc_end = SSD_INNER + CONV_DIM
    dt_end = xbc_end + SSD_HEADS
    o_ref[0, :, 0:xbc_end] = w[:, 0:xbc_end].astype(BF16)
    o_ref[0, :, xbc_end:xbc_end + POOL_WIDTH] = w[:, dt_end:dt_end + POOL_WIDTH].astype(BF16)
    dt = w[:, xbc_end:dt_end]
    zeros = jnp.zeros((w.shape[0], DT_LANES - DT_COPIES * SSD_HEADS), F32)
    o_ref[0, :, IN_COLS - DT_LANES:IN_COLS] = jnp.concatenate([dt] * DT_COPIES + [zeros], axis=1).astype(BF16)


def _relayout_w_in(w_in):
    depth, k, n = w_in.shape
    rb = W_IN_ROW_BLOCK
    return pl.pallas_call(
        _w_in_relayout_kernel,
        grid=(depth, k // rb),
        in_specs=[pl.BlockSpec((1, rb, n), lambda l, r: (l, r, 0))],
        out_specs=pl.BlockSpec((1, rb, IN_COLS), lambda l, r: (l, r, 0)),
        out_shape=jax.ShapeDtypeStruct((depth, k, IN_COLS), BF16),
        compiler_params=pltpu.CompilerParams(
            dimension_semantics=("parallel", "parallel"), vmem_limit_bytes=VMEM_LIMIT_BYTES),
        name="w_in_relayout",
    )(w_in)


def _prep_mixer_params(mix_norm, w_in, conv_w, conv_b, dt_bias, a_log, d_skip, ssd_norm, pool_w, pool_scale,
                       w_out_mix):
    depth = w_in.shape[0]
    n_pad = DT_LANES - DT_COPIES * SSD_HEADS
    win = _relayout_w_in(w_in)
    pad = jnp.zeros((depth, n_pad), F32)

    def row(a):
        return jnp.pad(a, ((0, 0), (0, CONV_DIM - a.shape[1])))[:, None, :]

    rowp = jnp.concatenate([
        row(mix_norm), conv_w, row(conv_b),
        row(jnp.concatenate([dt_bias] * DT_COPIES + [pad], axis=1)),
        row(jnp.concatenate([a_log] * DT_COPIES + [pad], axis=1)),
        row(jnp.repeat(d_skip, SSD_HEAD_DIM, axis=1)), row(ssd_norm), row(pool_scale),
        jnp.zeros((depth, ROW_PARAM_ROWS - ROW_POOL_SCALE - 1, CONV_DIM), F32)], axis=1)
    return rowp, win, w_out_mix.astype(BF16), pool_w.astype(BF16)


def _mixer(h, layer, params):
    bsz, s, _ = h.shape
    ts = SEQ_BLOCK
    nch = ts // CHUNK
    consts = _mixer_constants()
    in_specs = [pl.BlockSpec((1, ts, D_MODEL), lambda b, j: (b, j, 0))]
    in_specs += [_layer_spec(a.shape, layer) for a in params]
    in_specs += [_const_spec(consts.shape)]
    return pl.pallas_call(
        _mixer_kernel,
        grid=(bsz, s // ts),
        in_specs=in_specs,
        out_specs=pl.BlockSpec((1, ts, D_MODEL), lambda b, j: (b, j, 0)),
        out_shape=jax.ShapeDtypeStruct(h.shape, F32),
        scratch_shapes=[
            pltpu.VMEM((ts, SSD_INNER), F32),
            pltpu.VMEM((CONV_HALO, CONV_DIM), F32),
            pltpu.VMEM((ts, CONV_DIM), F32),
            pltpu.VMEM((ts, POOL_WIDTH), F32),
            pltpu.VMEM((CHUNK + ts, POOL_WIDTH), BF16),
            pltpu.VMEM((SSD_GROUPS, SSD_STATE, GROUP_WIDTH), F32),
            pltpu.VMEM((nch, CHUNK, CHUNK), F32),
            pltpu.VMEM((ts, SSD_INNER), F32),
            pltpu.VMEM((ts, SSD_INNER), BF16),
            pltpu.VMEM((ts, SSD_INNER), BF16),
            pltpu.VMEM((ts, SSD_INNER), BF16),
            pltpu.VMEM((nch, SSD_GROUPS, SSD_STATE, CHUNK), BF16),
            pltpu.VMEM((nch, SSD_GROUPS, CHUNK, SSD_STATE), BF16),
            pltpu.VMEM((nch, CHUNK, CHUNK), F32),
            pltpu.VMEM((ts, SSD_INNER), F32),
            pltpu.VMEM((ts, SSD_INNER), F32),
            pltpu.VMEM((ts, D_MIX), BF16),
            pltpu.VMEM((ts, D_MODEL), F32),
        ],
        compiler_params=pltpu.CompilerParams(
            dimension_semantics=("parallel", "arbitrary"), vmem_limit_bytes=VMEM_LIMIT_BYTES),
        name="mixer",
    )(h, *params, consts)


def _kv_kernel(mem_ref, g_ref, wkv_ref, kt_ref, v_ref):
    mn = _rms(mem_ref[0], g_ref[...]).astype(BF16)
    kv = _dot(mn, wkv_ref[...])
    kt_ref[0] = kv[:, 0:D_MODEL].T.astype(BF16)
    v_ref[0] = kv[:, D_MODEL:2 * D_MODEL].astype(BF16)


def _kv(mem, layer, g, w_kv):
    bsz = mem.shape[0]
    return pl.pallas_call(
        _kv_kernel,
        grid=(bsz,),
        in_specs=[pl.BlockSpec((1, MEM_LEN, D_MODEL), lambda b: (b, 0, 0)),
                  _layer_spec(g.shape, layer), _layer_spec(w_kv.shape, layer)],
        out_specs=[pl.BlockSpec((1, D_MODEL, MEM_LEN), lambda b: (b, 0, 0)),
                   pl.BlockSpec((1, MEM_LEN, D_MODEL), lambda b: (b, 0, 0))],
        out_shape=[jax.ShapeDtypeStruct((bsz, D_MODEL, MEM_LEN), BF16),
                   jax.ShapeDtypeStruct((bsz, MEM_LEN, D_MODEL), BF16)],
        compiler_params=pltpu.CompilerParams(
            dimension_semantics=("parallel",), vmem_limit_bytes=VMEM_LIMIT_BYTES),
        name="memory_kv",
    )(mem, g, w_kv)


def _xattn_ffn_kernel(h_ref, gx_ref, wq_ref, kt_ref, v_ref, wo_ref, gf_ref, wgu_ref, wd_ref, gfin_ref, out_ref,
                      o_s, a_s, h2_s, u2_s, *, final_norm):
    i = pl.program_id(0)

    @pl.when(i == 0)
    def _():
        h2_s[...] = jnp.zeros_like(h2_s)
        u2_s[...] = jnp.zeros_like(u2_s)

    head_cols = [slice(hd * XATTN_HEAD_DIM, (hd + 1) * XATTN_HEAD_DIM) for hd in range(XATTN_HEADS)]
    n_ff = D_FF // FF_COL_CHUNK
    x = h_ref[0]
    xp = h2_s[...]
    up_ = u2_s[...]

    def ffn_dots(ci):
        lo = ci * FF_COL_CHUNK
        return (_dot(up_, wgu_ref[:, lo:lo + FF_COL_CHUNK]),
                _dot(up_, wgu_ref[:, D_FF + lo:D_FF + lo + FF_COL_CHUNK]))

    def ffn_act(ci, gate_up):
        lo = ci * FF_COL_CHUNK
        a_s[:, lo:lo + FF_COL_CHUNK] = (_silu(gate_up[0]) * gate_up[1]).astype(BF16)

    gu = ffn_dots(0)
    u = _rms(x, gx_ref[...]).astype(BF16)
    q = (_dot(u, wq_ref[...]) * (XATTN_HEAD_DIM ** -0.5)).astype(BF16)
    ffn_act(0, gu)
    scores = [_dot(q[:, cols], kt_ref[0, cols, :]) for cols in head_cols]
    for ci in range(1, n_ff):
        gu = ffn_dots(ci)
        if ci == 1:
            ps = []
            for s in scores:
                p = jnp.exp(s - jnp.max(s, axis=-1, keepdims=True))
                ps.append((p.astype(BF16), jnp.sum(p, axis=-1, keepdims=True)))
        ffn_act(ci, gu)
    for cols, (p, den) in zip(head_cols, ps):
        o = _dot(p, v_ref[0, :, cols])
        o_s[:, cols] = (o * (1.0 / den)).astype(BF16)
    x2 = x + _dot(o_s[...], wo_ref[...])
    y = xp + _dot(a_s[...], wd_ref[...])
    h2_s[...] = x2
    u2_s[...] = _rms(x2, gf_ref[...]).astype(BF16)
    if final_norm:
        y = _rms(y, gfin_ref[...])
    out_ref[0] = y


def _xattn_ffn(h, layer, gx, w_q, kt, v, w_o, gf, w_gate_up, w_down, g_final, final_norm):
    bsz, s, _ = h.shape
    tq = SEQ_BLOCK
    nps = s // tq
    n_blocks = bsz * nps

    def cur(i):
        blk = jnp.minimum(i, n_blocks - 1)
        return (blk // nps, blk % nps, 0)

    def prev(i):
        blk = jnp.maximum(i - 1, 0)
        return (blk // nps, blk % nps, 0)

    def mem_of_cur(i):
        return (jnp.minimum(i, n_blocks - 1) // nps, 0, 0)

    return pl.pallas_call(
        functools.partial(_xattn_ffn_kernel, final_norm=final_norm),
        grid=(n_blocks + 1,),
        in_specs=[pl.BlockSpec((1, tq, D_MODEL), cur),
                  _layer_spec(gx.shape, layer), _layer_spec(w_q.shape, layer),
                  pl.BlockSpec((1, D_MODEL, MEM_LEN), mem_of_cur),
                  pl.BlockSpec((1, MEM_LEN, D_MODEL), mem_of_cur),
                  _layer_spec(w_o.shape, layer), _layer_spec(gf.shape, layer),
                  _layer_spec(w_gate_up.shape, layer), _layer_spec(w_down.shape, layer),
                  _const_spec((1, D_MODEL))],
        out_specs=pl.BlockSpec((1, tq, D_MODEL), prev),
        out_shape=jax.ShapeDtypeStruct(h.shape, F32),
        scratch_shapes=[pltpu.VMEM((tq, D_MODEL), BF16), pltpu.VMEM((tq, D_FF), BF16),
                        pltpu.VMEM((tq, D_MODEL), F32), pltpu.VMEM((tq, D_MODEL), BF16)],
        compiler_params=pltpu.CompilerParams(
            dimension_semantics=("arbitrary",), vmem_limit_bytes=VMEM_LIMIT_BYTES),
        name="xattn_swiglu_final" if final_norm else "xattn_swiglu",
    )(h, gx, w_q, kt, v, w_o, gf, w_gate_up, w_down, g_final.reshape(1, -1))


def kernel(x, mem, mix_norm, w_in, conv_w, conv_b, dt_bias, a_log, d_skip, ssd_norm, pool_w, pool_scale,
           w_out_mix, xattn_norm, mem_norm, w_q, w_kv, w_o, ffn_norm, w_gate_up, w_down, final_norm):
    depth = w_in.shape[0]
    row = lambda a: a.reshape(depth, 1, -1)
    mixer_params = _prep_mixer_params(mix_norm, w_in, conv_w, conv_b, dt_bias, a_log, d_skip, ssd_norm, pool_w,
                                      pool_scale, w_out_mix)
    xattn_g, mem_g, ffn_g = row(xattn_norm), row(mem_norm), row(ffn_norm)
    wq_b, wkv_b, wo_b = w_q.astype(BF16), w_kv.astype(BF16), w_o.astype(BF16)
    wgu_b, wd_b = w_gate_up.astype(BF16), w_down.astype(BF16)
    h = x
    for l in range(depth):
        h = _mixer(h, l, mixer_params)
        kt, v = _kv(mem, l, mem_g, wkv_b)
        h = _xattn_ffn(h, l, xattn_g, wq_b, kt, v, wo_b, ffn_g, wgu_b, wd_b, final_norm,
                       final_norm=(l == depth - 1))
    return h
```

```python
import functools

import numpy as np
import jax
import jax.numpy as jnp
from jax import lax
from jax.experimental import pallas as pl
from jax.experimental.pallas import tpu as pltpu

F32 = jnp.float32
BF16 = jnp.bfloat16

D_MODEL = 1024
SSD_INNER = 1024
SSD_HEADS = 16
SSD_HEAD_DIM = 64
SSD_GROUPS = 2
SSD_STATE = 128
GROUP_WIDTH = SSD_INNER // SSD_GROUPS
CONV_WIDTH = 4
CONV_DIM = SSD_INNER + 2 * SSD_GROUPS * SSD_STATE
CHUNK = 128
POOL_WINDOWS = (2, 4, 8, 16)
POOL_WIDTH = 1024
POOL_GROUP_DIM = 256
D_MIX = 2048
DT_LANES = 128
SPLIT_PIECES = 3
IN_COLS = SSD_INNER + CONV_DIM + POOL_WIDTH + DT_LANES
MEM_LEN = 256
XATTN_HEADS = 4
XATTN_HEAD_DIM = 256
D_FF = 2816
FF_COL_CHUNK = 1408
EPS = 1e-6
LOG2_E = 1.4426950408889634

ROW_MIX_NORM, ROW_CONV_W, ROW_CONV_B, ROW_DT_BIAS, ROW_A_LOG, ROW_D_SKIP, ROW_SSD_NORM, ROW_POOL_SCALE = (
    0, 1, 5, 6, 7, 8, 9, 10)
ROW_PARAM_ROWS = 16
CST_TRI = 0
CST_BAND = CST_TRI + SPLIT_PIECES * CHUNK

CONV_HALO = 8
CONV_COLS = 256
SEQ_BLOCK = 512
W_IN_ROW_BLOCK = 256
VMEM_LIMIT_BYTES = 56 * 1024 * 1024


def _rms(x, g):
    ms = jnp.mean(x * x, axis=-1, keepdims=True)
    return x * lax.rsqrt(ms + EPS) * g


def _silu(x):
    h = 0.5 * x
    return h + h * jnp.tanh(h)


def _softplus(x):
    return jnp.maximum(x, 0.0) + jnp.log1p(jnp.exp(-jnp.abs(x)))


def _dot(a, b):
    return jnp.dot(a, b, preferred_element_type=F32)


def _split3_f32(x):
    hi = x.astype(BF16).astype(F32)
    r1 = x - hi
    mid = r1.astype(BF16).astype(F32)
    return hi, mid, r1 - mid


def _mixer_kernel(h_ref, rowp_ref, win_ref, wout_ref, poolw_ref, cst_ref,
                  out_ref,
                  z_s, halo_s, xbcs_s, vf_s, vb_s, state_s, acst_s, eacs_s, xdec_s, xlo_s, xhi_s,
                  bmt_s, cmb_s, cb_s, y_s, yoff_s, ycat_s, acc_s):
    ts = z_s.shape[0]
    nch = ts // CHUNK
    j = pl.program_id(1)

    @pl.when(j == 0)
    def _():
        state_s[...] = jnp.zeros_like(state_s)
        halo_s[...] = jnp.zeros_like(halo_s)
        vb_s[0:CHUNK, :] = jnp.zeros((CHUNK, POOL_WIDTH), BF16)

    def rows(c):
        return slice(c * CHUNK, (c + 1) * CHUNK)

    g_ref = rowp_ref.at[ROW_MIX_NORM:ROW_MIX_NORM + 1, 0:D_MODEL]
    convw_ref = rowp_ref.at[ROW_CONV_W:ROW_CONV_W + CONV_WIDTH, :]
    convb_ref = rowp_ref.at[ROW_CONV_B:ROW_CONV_B + 1, :]
    dtb_ref = rowp_ref.at[ROW_DT_BIAS:ROW_DT_BIAS + 1, 0:DT_LANES]
    alog_ref = rowp_ref.at[ROW_A_LOG:ROW_A_LOG + 1, 0:DT_LANES]
    dskip_ref = rowp_ref.at[ROW_D_SKIP:ROW_D_SKIP + 1, 0:SSD_INNER]
    ssdn_ref = rowp_ref.at[ROW_SSD_NORM:ROW_SSD_NORM + 1, 0:SSD_INNER]
    pscale_ref = rowp_ref.at[ROW_POOL_SCALE:ROW_POOL_SCALE + 1, 0:POOL_WIDTH]
    tri_ref = cst_ref.at[:, CST_TRI:CST_TRI + SPLIT_PIECES * CHUNK]

    def band(gi):
        return cst_ref[:, CST_BAND + gi * 2 * CHUNK:CST_BAND + (gi + 1) * 2 * CHUNK]

    x = h_ref[0]
    u = _rms(x, g_ref[...]).astype(BF16)

    v_off = SSD_INNER + CONV_DIM

    def z_piece(i):
        z_s[:, i * CONV_COLS:(i + 1) * CONV_COLS] = _dot(u, win_ref[:, i * CONV_COLS:(i + 1) * CONV_COLS])

    def v_piece(i):
        v = _dot(u, win_ref[:, v_off + i * CONV_COLS:v_off + (i + 1) * CONV_COLS])
        vf_s[:, i * CONV_COLS:(i + 1) * CONV_COLS] = v
        vb_s[CHUNK:CHUNK + ts, i * CONV_COLS:(i + 1) * CONV_COLS] = v.astype(BF16)

    def conv_group(cg):
        cols = slice(cg * CONV_COLS, (cg + 1) * CONV_COLS)
        res = _dot(u, win_ref[:, SSD_INNER + cg * CONV_COLS:SSD_INNER + (cg + 1) * CONV_COLS])
        w = [convw_ref[k:k + 1, cols] for k in range(CONV_WIDTH)]
        ext = jnp.concatenate([halo_s[:, cols], res], axis=0)
        prev = pltpu.roll(ext, 1, axis=0)
        b = w[1] * ext + w[0] * prev
        a = convb_ref[:, cols] + w[3] * res + w[2] * prev[CONV_HALO:, :]
        xbcs_s[:, cols] = _silu(a + pltpu.roll(b, 2, axis=0)[CONV_HALO:, :])
        halo_s[:, cols] = res[ts - CONV_HALO:ts, :]

    lane = lax.broadcasted_iota(jnp.int32, (CHUNK, CHUNK), 1)
    row = lax.broadcasted_iota(jnp.int32, (CHUNK, CHUNK), 0)
    causal = row >= lane
    lane_b = lax.broadcasted_iota(jnp.int32, (ts, DT_LANES), 1)
    a_row = jnp.where(lane[0:1, :] < SSD_HEADS, -jnp.exp(alog_ref[...]), 0.0)

    tpos0 = j * ts + 1 + lax.broadcasted_iota(jnp.int32, (CHUNK, POOL_GROUP_DIM), 0)

    def pool_group(gi):
        w = POOL_WINDOWS[gi]
        cols = slice(gi * POOL_GROUP_DIM, (gi + 1) * POOL_GROUP_DIM)
        ds = []
        for c in range(nch):
            win = _dot(band(gi), vb_s[c * CHUNK:(c + 2) * CHUNK, cols])
            if c == 0:
                mean = win / jnp.minimum(tpos0.astype(F32), float(w))
            else:
                mean = win * (1.0 / w)
            ds.append((mean - vf_s[rows(c), cols]).astype(BF16))
        yp = _dot(jnp.concatenate(ds, axis=0), poolw_ref[gi]) * pscale_ref[:, cols]
        ycat_s[:, SSD_INNER + gi * POOL_GROUP_DIM:SSD_INNER + (gi + 1) * POOL_GROUP_DIM] = yp.astype(BF16)

    def pool_out_piece(k):
        cols = slice(k * CONV_COLS, (k + 1) * CONV_COLS)
        acc_s[:, cols] = x[:, cols] + _dot(ycat_s[:, SSD_INNER:D_MIX], wout_ref[SSD_INNER:D_MIX, cols])

    n_side = SSD_INNER // CONV_COLS
    fillers = ([functools.partial(v_piece, k) for k in range(n_side)]
               + [functools.partial(pool_group, gi) for gi in range(len(POOL_WINDOWS))]
               + [functools.partial(pool_out_piece, k) for k in range(n_side)]
               + [functools.partial(z_piece, k) for k in range(n_side)])

    def fill(n):
        for _ in range(n):
            if fillers:
                fillers.pop(0)()

    dt = _softplus(_dot(u, win_ref[:, IN_COLS - DT_LANES:IN_COLS]) + dtb_ref[...])
    adt_parts = _split3_f32(dt * a_row)
    fill(2)
    conv_group(0)
    acs_c = []
    for c in range(nch):
        stacked = jnp.concatenate([p[rows(c)] for p in adt_parts], axis=0).astype(BF16)
        acs_c.append(_dot(tri_ref[...], stacked) * LOG2_E)
    fill(2)
    conv_group(1)
    for c in range(nch):
        acst_s[c] = acs_c[c].T
    vb_tail = vb_s[ts:ts + CHUNK, :]
    fill(1)
    conv_group(2)
    def expand_heads(a):
        tiles = [jnp.where(lane_b < SSD_HEAD_DIM,
                           jnp.broadcast_to(a[:, 2 * p:2 * p + 1], (ts, DT_LANES)),
                           jnp.broadcast_to(a[:, 2 * p + 1:2 * p + 2], (ts, DT_LANES)))
                 for p in range(SSD_HEADS // 2)]
        return jnp.concatenate(tiles, axis=1)

    acs_x = expand_heads(jnp.concatenate(acs_c, axis=0))
    dt_x = expand_heads(dt)
    fill(1)
    conv_group(3)
    eacs_s[...] = jnp.exp2(acs_x)
    fill(2)
    lane_x = lax.broadcasted_iota(jnp.int32, (CHUNK, SSD_INNER), 1) % (2 * SSD_HEAD_DIM)
    for c in range(nch):
        r = rows(c)
        xc = xbcs_s[r, 0:SSD_INNER] * dt_x[r]
        alast = acs_x[(c + 1) * CHUNK - 1:(c + 1) * CHUNK, :]
        xdec_s[r, :] = (xc * jnp.exp2(alast - acs_x[r])).astype(BF16)
        xlo_s[r, :] = jnp.where(lane_x < SSD_HEAD_DIM, xc, 0.0).astype(BF16)
        xhi_s[r, :] = jnp.where(lane_x >= SSD_HEAD_DIM, xc, 0.0).astype(BF16)
        fill(1)
        if c == 1:
            conv_group(4)
    conv_group(5)
    fill(1)
    for c in range(nch):
        for g in range(SSD_GROUPS):
            b_cols = slice(SSD_INNER + g * SSD_STATE, SSD_INNER + (g + 1) * SSD_STATE)
            c_cols = slice(SSD_INNER + (SSD_GROUPS + g) * SSD_STATE, SSD_INNER + (SSD_GROUPS + g + 1) * SSD_STATE)
            bmt_s[c, g] = xbcs_s[rows(c), b_cols].T.astype(BF16)
            cmb_s[c, g] = xbcs_s[rows(c), c_cols].astype(BF16)

    pairs_per_group = SSD_HEADS // SSD_GROUPS // 2
    for p in range(SSD_HEADS // 2):
        g = p // pairs_per_group
        pcols = slice(p * 2 * SSD_HEAD_DIM, (p + 1) * 2 * SSD_HEAD_DIM)
        if p % pairs_per_group == 0:
            for c in range(nch):
                cb_s[c] = _dot(cmb_s[c, g], bmt_s[c, g])
        for c in range(nch):
            ms = []
            for q in range(2):
                hh = 2 * p + q
                dmat = jnp.broadcast_to(acs_c[c][:, hh:hh + 1], (CHUNK, CHUNK)) - acst_s[c, hh:hh + 1, :]
                seg = jnp.exp2(jnp.where(causal, dmat, -jnp.inf))
                ms.append((cb_s[c] * seg).astype(BF16))
            rhs = jnp.concatenate([xlo_s[rows(c), pcols], xhi_s[rows(c), pcols]], axis=0)
            y_s[rows(c), pcols] = _dot(jnp.concatenate(ms, axis=1), rhs)
        if p < 1:
            fill(1)
    vb_s[0:CHUNK, :] = vb_tail

    new_st = {}
    for c in range(nch):
        for g in range(SSD_GROUPS):
            gcols = slice(g * GROUP_WIDTH, (g + 1) * GROUP_WIDTH)
            new_st[c, g] = _dot(bmt_s[c, g], xdec_s[rows(c), gcols])
    fill(len(fillers))
    for c in range(nch):
        for g in range(SSD_GROUPS):
            gcols = slice(g * GROUP_WIDTH, (g + 1) * GROUP_WIDTH)
            st = state_s[g]
            yoff_s[rows(c), gcols] = _dot(cmb_s[c, g], st.astype(BF16))
            state_s[g] = st * eacs_s[(c + 1) * CHUNK - 1:(c + 1) * CHUNK, gcols] + new_st[c, g]

    for c in range(nch):
        r = rows(c)
        y = y_s[r, :] + yoff_s[r, :] * eacs_s[r, :] + dskip_ref[...] * xbcs_s[r, 0:SSD_INNER]
        y = y * _silu(z_s[r, :])
        ycat_s[r, 0:SSD_INNER] = _rms(y, ssdn_ref[...]).astype(BF16)
        out_ref[0, r, :] = acc_s[r, :] + _dot(ycat_s[r, 0:SSD_INNER], wout_ref[0:SSD_INNER, :])


def _const_spec(shape):
    nd = len(shape)
    return pl.BlockSpec(shape, lambda *_: (0,) * nd, pipeline_mode=pl.Buffered(1))


def _layer_spec(shape, layer):
    nd = len(shape)
    return pl.BlockSpec((None,) + tuple(shape[1:]), lambda *_: (layer,) + (0,) * (nd - 1),
                        pipeline_mode=pl.Buffered(1))


def _mixer_constants():
    t = np.arange(CHUNK)
    tri = (t[:, None] >= t[None, :]).astype(np.float32)
    tri3 = np.concatenate([tri] * SPLIT_PIECES, axis=1)
    k = np.arange(2 * CHUNK) - CHUNK
    band = [((k[None, :] <= t[:, None]) & (k[None, :] >= t[:, None] - w + 1)).astype(np.float32)
            for w in POOL_WINDOWS]
    return jnp.asarray(np.concatenate([tri3] + band, axis=1), BF16)


def _w_in_relayout_kernel(w_ref, o_ref):
    w = w_ref[0]
    xbc_end = SSD_INNER + CONV_DIM
    dt_end = xbc_end + SSD_HEADS
    o_ref[0, :, 0:xbc_end] = w[:, 0:xbc_end].astype(BF16)
    o_ref[0, :, xbc_end:xbc_end + POOL_WIDTH] = w[:, dt_end:dt_end + POOL_WIDTH].astype(BF16)
    dt = w[:, xbc_end:dt_end]
    zeros = jnp.zeros((w.shape[0], DT_LANES - SSD_HEADS), F32)
    o_ref[0, :, IN_COLS - DT_LANES:IN_COLS] = jnp.concatenate([dt, zeros], axis=1).astype(BF16)


def _relayout_w_in(w_in):
    depth, k, n = w_in.shape
    rb = W_IN_ROW_BLOCK
    return pl.pallas_call(
        _w_in_relayout_kernel,
        grid=(depth, k // rb),
        in_specs=[pl.BlockSpec((1, rb, n), lambda l, r: (l, r, 0))],
        out_specs=pl.BlockSpec((1, rb, IN_COLS), lambda l, r: (l, r, 0)),
        out_shape=jax.ShapeDtypeStruct((depth, k, IN_COLS), BF16),
        compiler_params=pltpu.CompilerParams(
            dimension_semantics=("parallel", "parallel"), vmem_limit_bytes=VMEM_LIMIT_BYTES),
        name="w_in_relayout",
    )(w_in)


def _prep_mixer_params(mix_norm, w_in, conv_w, conv_b, dt_bias, a_log, d_skip, ssd_norm, pool_w, pool_scale,
                       w_out_mix):
    depth = w_in.shape[0]
    n_pad = DT_LANES - SSD_HEADS
    win = _relayout_w_in(w_in)
    pad = jnp.zeros((depth, n_pad), F32)

    def row(a):
        return jnp.pad(a, ((0, 0), (0, CONV_DIM - a.shape[1])))[:, None, :]

    rowp = jnp.concatenate([
        row(mix_norm), conv_w, row(conv_b),
        row(jnp.concatenate([dt_bias, pad], axis=1)),
        row(jnp.concatenate([a_log, pad], axis=1)),
        row(jnp.repeat(d_skip, SSD_HEAD_DIM, axis=1)), row(ssd_norm), row(pool_scale),
        jnp.zeros((depth, ROW_PARAM_ROWS - ROW_POOL_SCALE - 1, CONV_DIM), F32)], axis=1)
    return rowp, win, w_out_mix.astype(BF16), pool_w.astype(BF16)


def _mixer(h, layer, params):
    bsz, s, _ = h.shape
    ts = SEQ_BLOCK
    nch = ts // CHUNK
    consts = _mixer_constants()
    in_specs = [pl.BlockSpec((1, ts, D_MODEL), lambda b, j: (b, j, 0))]
    in_specs += [_layer_spec(a.shape, layer) for a in params]
    in_specs += [_const_spec(consts.shape)]
    return pl.pallas_call(
        _mixer_kernel,
        grid=(bsz, s // ts),
        in_specs=in_specs,
        out_specs=pl.BlockSpec((1, ts, D_MODEL), lambda b, j: (b, j, 0)),
        out_shape=jax.ShapeDtypeStruct(h.shape, F32),
        scratch_shapes=[
            pltpu.VMEM((ts, SSD_INNER), F32),
            pltpu.VMEM((CONV_HALO, CONV_DIM), F32),
            pltpu.VMEM((ts, CONV_DIM), F32),
            pltpu.VMEM((ts, POOL_WIDTH), F32),
            pltpu.VMEM((CHUNK + ts, POOL_WIDTH), BF16),
            pltpu.VMEM((SSD_GROUPS, SSD_STATE, GROUP_WIDTH), F32),
            pltpu.VMEM((nch, CHUNK, CHUNK), F32),
            pltpu.VMEM((ts, SSD_INNER), F32),
            pltpu.VMEM((ts, SSD_INNER), BF16),
            pltpu.VMEM((ts, SSD_INNER), BF16),
            pltpu.VMEM((ts, SSD_INNER), BF16),
            pltpu.VMEM((nch, SSD_GROUPS, SSD_STATE, CHUNK), BF16),
            pltpu.VMEM((nch, SSD_GROUPS, CHUNK, SSD_STATE), BF16),
            pltpu.VMEM((nch, CHUNK, CHUNK), F32),
            pltpu.VMEM((ts, SSD_INNER), F32),
            pltpu.VMEM((ts, SSD_INNER), F32),
            pltpu.VMEM((ts, D_MIX), BF16),
            pltpu.VMEM((ts, D_MODEL), F32),
        ],
        compiler_params=pltpu.CompilerParams(
            dimension_semantics=("parallel", "arbitrary"), vmem_limit_bytes=VMEM_LIMIT_BYTES),
        name="mixer",
    )(h, *params, consts)


def _kv_kernel(mem_ref, g_ref, wkv_ref, kt_ref, v_ref):
    mn = _rms(mem_ref[0], g_ref[...]).astype(BF16)
    kv = _dot(mn, wkv_ref[...])
    kt_ref[0] = kv[:, 0:D_MODEL].T.astype(BF16)
    v_ref[0] = kv[:, D_MODEL:2 * D_MODEL].astype(BF16)


def _kv(mem, layer, g, w_kv):
    bsz = mem.shape[0]
    return pl.pallas_call(
        _kv_kernel,
        grid=(bsz,),
        in_specs=[pl.BlockSpec((1, MEM_LEN, D_MODEL), lambda b: (b, 0, 0)),
                  _layer_spec(g.shape, layer), _layer_spec(w_kv.shape, layer)],
        out_specs=[pl.BlockSpec((1, D_MODEL, MEM_LEN), lambda b: (b, 0, 0)),
                   pl.BlockSpec((1, MEM_LEN, D_MODEL), lambda b: (b, 0, 0))],
        out_shape=[jax.ShapeDtypeStruct((bsz, D_MODEL, MEM_LEN), BF16),
                   jax.ShapeDtypeStruct((bsz, MEM_LEN, D_MODEL), BF16)],
        compiler_params=pltpu.CompilerParams(
            dimension_semantics=("parallel",), vmem_limit_bytes=VMEM_LIMIT_BYTES),
        name="memory_kv",
    )(mem, g, w_kv)


def _xattn_ffn_kernel(h_ref, gx_ref, wq_ref, kt_ref, v_ref, wo_ref, gf_ref, wgu_ref, wd_ref, gfin_ref, out_ref,
                      o_s, a_s, h2_s, u2_s, *, final_norm):
    i = pl.program_id(0)

    @pl.when(i == 0)
    def _():
        h2_s[...] = jnp.zeros_like(h2_s)
        u2_s[...] = jnp.zeros_like(u2_s)

    head_cols = [slice(hd * XATTN_HEAD_DIM, (hd + 1) * XATTN_HEAD_DIM) for hd in range(XATTN_HEADS)]
    n_ff = D_FF // FF_COL_CHUNK
    x = h_ref[0]
    xp = h2_s[...]
    up_ = u2_s[...]

    def ffn_dots(ci):
        lo = ci * FF_COL_CHUNK
        return (_dot(up_, wgu_ref[:, lo:lo + FF_COL_CHUNK]),
                _dot(up_, wgu_ref[:, D_FF + lo:D_FF + lo + FF_COL_CHUNK]))

    def ffn_act(ci, gate_up):
        lo = ci * FF_COL_CHUNK
        a_s[:, lo:lo + FF_COL_CHUNK] = (_silu(gate_up[0]) * gate_up[1]).astype(BF16)

    gu = ffn_dots(0)
    u = _rms(x, gx_ref[...]).astype(BF16)
    q = (_dot(u, wq_ref[...]) * (XATTN_HEAD_DIM ** -0.5)).astype(BF16)
    ffn_act(0, gu)
    scores = [_dot(q[:, cols], kt_ref[0, cols, :]) for cols in head_cols]
    for ci in range(1, n_ff):
        gu = ffn_dots(ci)
        if ci == 1:
            ps = []
            for s in scores:
                p = jnp.exp(s - jnp.max(s, axis=-1, keepdims=True))
                ps.append((p.astype(BF16), jnp.sum(p, axis=-1, keepdims=True)))
        ffn_act(ci, gu)
    for cols, (p, den) in zip(head_cols, ps):
        o = _dot(p, v_ref[0, :, cols])
        o_s[:, cols] = (o * (1.0 / den)).astype(BF16)
    x2 = x + _dot(o_s[...], wo_ref[...])
    y = xp + _dot(a_s[...], wd_ref[...])
    h2_s[...] = x2
    u2_s[...] = _rms(x2, gf_ref[...]).astype(BF16)
    if final_norm:
        y = _rms(y, gfin_ref[...])
    out_ref[0] = y


def _xattn_ffn(h, layer, gx, w_q, kt, v, w_o, gf, w_gate_up, w_down, g_final, final_norm):
    bsz, s, _ = h.shape
    tq = SEQ_BLOCK
    nps = s // tq
    n_blocks = bsz * nps

    def cur(i):
        blk = jnp.minimum(i, n_blocks - 1)
        return (blk // nps, blk % nps, 0)

    def prev(i):
        blk = jnp.maximum(i - 1, 0)
        return (blk // nps, blk % nps, 0)

    def mem_of_cur(i):
        return (jnp.minimum(i, n_blocks - 1) // nps, 0, 0)

    return pl.pallas_call(
        functools.partial(_xattn_ffn_kernel, final_norm=final_norm),
        grid=(n_blocks + 1,),
        in_specs=[pl.BlockSpec((1, tq, D_MODEL), cur),
                  _layer_spec(gx.shape, layer), _layer_spec(w_q.shape, layer),
                  pl.BlockSpec((1, D_MODEL, MEM_LEN), mem_of_cur),
                  pl.BlockSpec((1, MEM_LEN, D_MODEL), mem_of_cur),
                  _layer_spec(w_o.shape, layer), _layer_spec(gf.shape, layer),
                  _layer_spec(w_gate_up.shape, layer), _layer_spec(w_down.shape, layer),
                  _const_spec((1, D_MODEL))],
        out_specs=pl.BlockSpec((1, tq, D_MODEL), prev),
        out_shape=jax.ShapeDtypeStruct(h.shape, F32),
        scratch_shapes=[pltpu.VMEM((tq, D_MODEL), BF16), pltpu.VMEM((tq, D_FF), BF16),
                        pltpu.VMEM((tq, D_MODEL), F32), pltpu.VMEM((tq, D_MODEL), BF16)],
        compiler_params=pltpu.CompilerParams(
            dimension_semantics=("arbitrary",), vmem_limit_bytes=VMEM_LIMIT_BYTES),
        name="xattn_swiglu_final" if final_norm else "xattn_swiglu",
    )(h, gx, w_q, kt, v, w_o, gf, w_gate_up, w_down, g_final.reshape(1, -1))


def kernel(x, mem, mix_norm, w_in, conv_w, conv_b, dt_bias, a_log, d_skip, ssd_norm, pool_w, pool_scale,
           w_out_mix, xattn_norm, mem_norm, w_q, w_kv, w_o, ffn_norm, w_gate_up, w_down, final_norm):
    depth = w_in.shape[0]
    row = lambda a: a.reshape(depth, 1, -1)
    mixer_params = _prep_mixer_params(mix_norm, w_in, conv_w, conv_b, dt_bias, a_log, d_skip, ssd_norm, pool_w,
                                      pool_scale, w_out_mix)
    xattn_g, mem_g, ffn_g = row(xattn_norm), row(mem_norm), row(ffn_norm)
    wq_b, wkv_b, wo_b = w_q.astype(BF16), w_kv.astype(BF16), w_o.astype(BF16)
    wgu_b, wd_b = w_gate_up.astype(BF16), w_down.astype(BF16)
    h = x
    for l in range(depth):
        h = _mixer(h, l, mixer_params)
        kt, v = _kv(mem, l, mem_g, wkv_b)
        h = _xattn_ffn(h, l, xattn_g, wq_b, kt, v, wo_b, ffn_g, wgu_b, wd_b, final_norm,
                       final_norm=(l == depth - 1))
    return h
```

```python
import functools

import numpy as np
import jax
import jax.numpy as jnp
from jax import lax
from jax.experimental import pallas as pl
from jax.experimental.pallas import tpu as pltpu

F32 = jnp.float32
BF16 = jnp.bfloat16

D_MODEL = 1024
SSD_INNER = 1024
SSD_HEADS = 16
SSD_HEAD_DIM = 64
SSD_GROUPS = 2
SSD_STATE = 128
GROUP_WIDTH = SSD_INNER // SSD_GROUPS
CONV_WIDTH = 4
CONV_DIM = SSD_INNER + 2 * SSD_GROUPS * SSD_STATE
CHUNK = 128
POOL_WINDOWS = (2, 4, 8, 16)
POOL_WIDTH = 1024
POOL_GROUP_DIM = 256
D_MIX = 2048
DT_LANES = 128
SPLIT_PIECES = 3
IN_COLS = SSD_INNER + CONV_DIM + POOL_WIDTH + DT_LANES
MEM_LEN = 256
XATTN_HEADS = 4
XATTN_HEAD_DIM = 256
D_FF = 2816
FF_COL_CHUNK = 1408
EPS = 1e-6
LOG2_E = 1.4426950408889634

ROW_MIX_NORM, ROW_CONV_W, ROW_CONV_B, ROW_DT_BIAS, ROW_A_LOG, ROW_D_SKIP, ROW_SSD_NORM, ROW_POOL_SCALE = (
    0, 1, 5, 6, 7, 8, 9, 10)
ROW_PARAM_ROWS = 16
CST_TRI = 0
CST_BAND = CST_TRI + SPLIT_PIECES * CHUNK

CONV_HALO = 8
CONV_COLS = 256
SEQ_BLOCK = 512
W_IN_ROW_BLOCK = 256
VMEM_LIMIT_BYTES = 56 * 1024 * 1024


def _rms(x, g):
    ms = jnp.mean(x * x, axis=-1, keepdims=True)
    return x * lax.rsqrt(ms + EPS) * g


def _silu(x):
    h = 0.5 * x
    return h + h * jnp.tanh(h)


def _softplus(x):
    return jnp.maximum(x, 0.0) + jnp.log1p(jnp.exp(-jnp.abs(x)))


def _dot(a, b):
    return jnp.dot(a, b, preferred_element_type=F32)


def _split3_f32(x):
    hi = x.astype(BF16).astype(F32)
    r1 = x - hi
    mid = r1.astype(BF16).astype(F32)
    return hi, mid, r1 - mid


def _mixer_kernel(h_ref, rowp_ref, win_ref, wout_ref, poolw_ref, cst_ref,
                  out_ref,
                  z_s, halo_s, xbcs_s, vf_s, vb_s, state_s, acst_s, eacs_s, xdec_s, xlo_s, xhi_s,
                  bmt_s, cmb_s, cb_s, y_s, yoff_s, ycat_s, acc_s):
    ts = z_s.shape[0]
    nch = ts // CHUNK
    j = pl.program_id(1)

    @pl.when(j == 0)
    def _():
        state_s[...] = jnp.zeros_like(state_s)
        halo_s[...] = jnp.zeros_like(halo_s)
        vb_s[0:CHUNK, :] = jnp.zeros((CHUNK, POOL_WIDTH), BF16)

    def rows(c):
        return slice(c * CHUNK, (c + 1) * CHUNK)

    g_ref = rowp_ref.at[ROW_MIX_NORM:ROW_MIX_NORM + 1, 0:D_MODEL]
    convw_ref = rowp_ref.at[ROW_CONV_W:ROW_CONV_W + CONV_WIDTH, :]
    convb_ref = rowp_ref.at[ROW_CONV_B:ROW_CONV_B + 1, :]
    dtb_ref = rowp_ref.at[ROW_DT_BIAS:ROW_DT_BIAS + 1, 0:DT_LANES]
    alog_ref = rowp_ref.at[ROW_A_LOG:ROW_A_LOG + 1, 0:DT_LANES]
    dskip_ref = rowp_ref.at[ROW_D_SKIP:ROW_D_SKIP + 1, 0:SSD_INNER]
    ssdn_ref = rowp_ref.at[ROW_SSD_NORM:ROW_SSD_NORM + 1, 0:SSD_INNER]
    pscale_ref = rowp_ref.at[ROW_POOL_SCALE:ROW_POOL_SCALE + 1, 0:POOL_WIDTH]
    tri_ref = cst_ref.at[:, CST_TRI:CST_TRI + SPLIT_PIECES * CHUNK]

    def band(gi):
        return cst_ref[:, CST_BAND + gi * 2 * CHUNK:CST_BAND + (gi + 1) * 2 * CHUNK]

    x = h_ref[0]
    u = _rms(x, g_ref[...]).astype(BF16)

    v_off = SSD_INNER + CONV_DIM

    def z_piece(i):
        z_s[:, i * CONV_COLS:(i + 1) * CONV_COLS] = _dot(u, win_ref[:, i * CONV_COLS:(i + 1) * CONV_COLS])

    def v_piece(i):
        v = _dot(u, win_ref[:, v_off + i * CONV_COLS:v_off + (i + 1) * CONV_COLS])
        vf_s[:, i * CONV_COLS:(i + 1) * CONV_COLS] = v
        vb_s[CHUNK:CHUNK + ts, i * CONV_COLS:(i + 1) * CONV_COLS] = v.astype(BF16)

    def conv_group(cg):
        cols = slice(cg * CONV_COLS, (cg + 1) * CONV_COLS)
        res = _dot(u, win_ref[:, SSD_INNER + cg * CONV_COLS:SSD_INNER + (cg + 1) * CONV_COLS])
        w = [convw_ref[k:k + 1, cols] for k in range(CONV_WIDTH)]
        ext = jnp.concatenate([halo_s[:, cols], res], axis=0)
        prev = pltpu.roll(ext, 1, axis=0)
        b = w[1] * ext + w[0] * prev
        a = convb_ref[:, cols] + w[3] * res + w[2] * prev[CONV_HALO:, :]
        xbcs_s[:, cols] = _silu(a + pltpu.roll(b, 2, axis=0)[CONV_HALO:, :])
        halo_s[:, cols] = res[ts - CONV_HALO:ts, :]

    lane = lax.broadcasted_iota(jnp.int32, (CHUNK, CHUNK), 1)
    row = lax.broadcasted_iota(jnp.int32, (CHUNK, CHUNK), 0)
    causal = row >= lane
    lane_b = lax.broadcasted_iota(jnp.int32, (ts, DT_LANES), 1)
    a_row = jnp.where(lane[0:1, :] < SSD_HEADS, -jnp.exp(alog_ref[...]), 0.0)

    tpos0 = j * ts + 1 + lax.broadcasted_iota(jnp.int32, (CHUNK, POOL_GROUP_DIM), 0)

    def pool_group(gi):
        w = POOL_WINDOWS[gi]
        cols = slice(gi * POOL_GROUP_DIM, (gi + 1) * POOL_GROUP_DIM)
        ds = []
        for c in range(nch):
            win = _dot(band(gi), vb_s[c * CHUNK:(c + 2) * CHUNK, cols])
            if c == 0:
                mean = win / jnp.minimum(tpos0.astype(F32), float(w))
            else:
                mean = win * (1.0 / w)
            ds.append((mean - vf_s[rows(c), cols]).astype(BF16))
        yp = _dot(jnp.concatenate(ds, axis=0), poolw_ref[gi]) * pscale_ref[:, cols]
        ycat_s[:, SSD_INNER + gi * POOL_GROUP_DIM:SSD_INNER + (gi + 1) * POOL_GROUP_DIM] = yp.astype(BF16)

    def pool_out_piece(k):
        cols = slice(k * CONV_COLS, (k + 1) * CONV_COLS)
        acc_s[:, cols] = x[:, cols] + _dot(ycat_s[:, SSD_INNER:D_MIX], wout_ref[SSD_INNER:D_MIX, cols])

    n_side = SSD_INNER // CONV_COLS
    fillers = ([functools.partial(v_piece, k) for k in range(n_side)]
               + [functools.partial(pool_group, gi) for gi in range(len(POOL_WINDOWS))]
               + [functools.partial(pool_out_piece, k) for k in range(n_side)]
               + [functools.partial(z_piece, k) for k in range(n_side)])

    def fill(n):
        for _ in range(n):
            if fillers:
                fillers.pop(0)()

    dt = _softplus(_dot(u, win_ref[:, IN_COLS - DT_LANES:IN_COLS]) + dtb_ref[...])
    adt_parts = _split3_f32(dt * a_row)
    fill(2)
    conv_group(0)
    acs_c = []
    for c in range(nch):
        stacked = jnp.concatenate([p[rows(c)] for p in adt_parts], axis=0).astype(BF16)
        acs_c.append(_dot(tri_ref[...], stacked) * LOG2_E)
    fill(2)
    conv_group(1)
    for c in range(nch):
        acst_s[c] = acs_c[c].T
    vb_tail = vb_s[ts:ts + CHUNK, :]

    def expand_heads(a):
        tiles = [jnp.where(lane_b < SSD_HEAD_DIM,
                           jnp.broadcast_to(a[:, 2 * p:2 * p + 1], (ts, DT_LANES)),
                           jnp.broadcast_to(a[:, 2 * p + 1:2 * p + 2], (ts, DT_LANES)))
                 for p in range(SSD_HEADS // 2)]
        return jnp.concatenate(tiles, axis=1)

    acs_x = expand_heads(jnp.concatenate(acs_c, axis=0))
    dt_x = expand_heads(dt)
    fill(1)
    conv_group(2)
    fill(1)
    conv_group(3)
    eacs_s[...] = jnp.exp2(acs_x)
    fill(2)
    lane_x = lax.broadcasted_iota(jnp.int32, (CHUNK, SSD_INNER), 1) % (2 * SSD_HEAD_DIM)
    for c in range(nch):
        r = rows(c)
        xc = xbcs_s[r, 0:SSD_INNER] * dt_x[r]
        alast = acs_x[(c + 1) * CHUNK - 1:(c + 1) * CHUNK, :]
        xdec_s[r, :] = (xc * jnp.exp2(alast - acs_x[r])).astype(BF16)
        xlo_s[r, :] = jnp.where(lane_x < SSD_HEAD_DIM, xc, 0.0).astype(BF16)
        xhi_s[r, :] = jnp.where(lane_x >= SSD_HEAD_DIM, xc, 0.0).astype(BF16)
        fill(1)
        if c == 1:
            conv_group(4)
    conv_group(5)
    fill(1)
    for c in range(nch):
        for g in range(SSD_GROUPS):
            b_cols = slice(SSD_INNER + g * SSD_STATE, SSD_INNER + (g + 1) * SSD_STATE)
            c_cols = slice(SSD_INNER + (SSD_GROUPS + g) * SSD_STATE, SSD_INNER + (SSD_GROUPS + g + 1) * SSD_STATE)
            bmt_s[c, g] = xbcs_s[rows(c), b_cols].T.astype(BF16)
            cmb_s[c, g] = xbcs_s[rows(c), c_cols].astype(BF16)

    pairs_per_group = SSD_HEADS // SSD_GROUPS // 2
    for p in range(SSD_HEADS // 2):
        g = p // pairs_per_group
        pcols = slice(p * 2 * SSD_HEAD_DIM, (p + 1) * 2 * SSD_HEAD_DIM)
        if p % pairs_per_group == 0:
            for c in range(nch):
                cb_s[c] = _dot(cmb_s[c, g], bmt_s[c, g])
        for c in range(nch):
            ms = []
            for q in range(2):
                hh = 2 * p + q
                dmat = jnp.broadcast_to(acs_c[c][:, hh:hh + 1], (CHUNK, CHUNK)) - acst_s[c, hh:hh + 1, :]
                seg = jnp.exp2(jnp.where(causal, dmat, -jnp.inf))
                ms.append((cb_s[c] * seg).astype(BF16))
            rhs = jnp.concatenate([xlo_s[rows(c), pcols], xhi_s[rows(c), pcols]], axis=0)
            y_s[rows(c), pcols] = _dot(jnp.concatenate(ms, axis=1), rhs)
        if p < 1:
            fill(1)
    vb_s[0:CHUNK, :] = vb_tail

    new_st = {}
    for c in range(nch):
        for g in range(SSD_GROUPS):
            gcols = slice(g * GROUP_WIDTH, (g + 1) * GROUP_WIDTH)
            new_st[c, g] = _dot(bmt_s[c, g], xdec_s[rows(c), gcols])
    fill(len(fillers))
    for c in range(nch):
        for g in range(SSD_GROUPS):
            gcols = slice(g * GROUP_WIDTH, (g + 1) * GROUP_WIDTH)
            st = state_s[g]
            yoff_s[rows(c), gcols] = _dot(cmb_s[c, g], st.astype(BF16))
            state_s[g] = st * eacs_s[(c + 1) * CHUNK - 1:(c + 1) * CHUNK, gcols] + new_st[c, g]

    for c in range(nch):
        r = rows(c)
        y = y_s[r, :] + yoff_s[r, :] * eacs_s[r, :] + dskip_ref[...] * xbcs_s[r, 0:SSD_INNER]
        y = y * _silu(z_s[r, :])
        ycat_s[r, 0:SSD_INNER] = _rms(y, ssdn_ref[...]).astype(BF16)
        out_ref[0, r, :] = acc_s[r, :] + _dot(ycat_s[r, 0:SSD_INNER], wout_ref[0:SSD_INNER, :])


def _const_spec(shape):
    nd = len(shape)
    return pl.BlockSpec(shape, lambda *_: (0,) * nd, pipeline_mode=pl.Buffered(1))


def _layer_spec(shape, layer):
    nd = len(shape)
    return pl.BlockSpec((None,) + tuple(shape[1:]), lambda *_: (layer,) + (0,) * (nd - 1),
                        pipeline_mode=pl.Buffered(1))


def _mixer_constants():
    t = np.arange(CHUNK)
    tri = (t[:, None] >= t[None, :]).astype(np.float32)
    tri3 = np.concatenate([tri] * SPLIT_PIECES, axis=1)
    k = np.arange(2 * CHUNK) - CHUNK
    band = [((k[None, :] <= t[:, None]) & (k[None, :] >= t[:, None] - w + 1)).astype(np.float32)
            for w in POOL_WINDOWS]
    return jnp.asarray(np.concatenate([tri3] + band, axis=1), BF16)


def _w_in_relayout_kernel(w_ref, o_ref):
    w = w_ref[0]
    xbc_end = SSD_INNER + CONV_DIM
    dt_end = xbc_end + SSD_HEADS
    o_ref[0, :, 0:xbc_end] = w[:, 0:xbc_end].astype(BF16)
    o_ref[0, :, xbc_end:xbc_end + POOL_WIDTH] = w[:, dt_end:dt_end + POOL_WIDTH].astype(BF16)
    dt = w[:, xbc_end:dt_end]
    zeros = jnp.zeros((w.shape[0], DT_LANES - SSD_HEADS), F32)
    o_ref[0, :, IN_COLS - DT_LANES:IN_COLS] = jnp.concatenate([dt, zeros], axis=1).astype(BF16)


def _relayout_w_in(w_in):
    depth, k, n = w_in.shape
    rb = W_IN_ROW_BLOCK
    return pl.pallas_call(
        _w_in_relayout_kernel,
        grid=(depth, k // rb),
        in_specs=[pl.BlockSpec((1, rb, n), lambda l, r: (l, r, 0))],
        out_specs=pl.BlockSpec((1, rb, IN_COLS), lambda l, r: (l, r, 0)),
        out_shape=jax.ShapeDtypeStruct((depth, k, IN_COLS), BF16),
        compiler_params=pltpu.CompilerParams(
            dimension_semantics=("parallel", "parallel"), vmem_limit_bytes=VMEM_LIMIT_BYTES),
        name="w_in_relayout",
    )(w_in)


def _prep_mixer_params(mix_norm, w_in, conv_w, conv_b, dt_bias, a_log, d_skip, ssd_norm, pool_w, pool_scale,
                       w_out_mix):
    depth = w_in.shape[0]
    n_pad = DT_LANES - SSD_HEADS
    win = _relayout_w_in(w_in)
    pad = jnp.zeros((depth, n_pad), F32)

    def row(a):
        return jnp.pad(a, ((0, 0), (0, CONV_DIM - a.shape[1])))[:, None, :]

    rowp = jnp.concatenate([
        row(mix_norm), conv_w, row(conv_b),
        row(jnp.concatenate([dt_bias, pad], axis=1)),
        row(jnp.concatenate([a_log, pad], axis=1)),
        row(jnp.repeat(d_skip, SSD_HEAD_DIM, axis=1)), row(ssd_norm), row(pool_scale),
        jnp.zeros((depth, ROW_PARAM_ROWS - ROW_POOL_SCALE - 1, CONV_DIM), F32)], axis=1)
    return rowp, win, w_out_mix.astype(BF16), pool_w.astype(BF16)


def _mixer(h, layer, params):
    bsz, s, _ = h.shape
    ts = SEQ_BLOCK
    nch = ts // CHUNK
    consts = _mixer_constants()
    in_specs = [pl.BlockSpec((1, ts, D_MODEL), lambda b, j: (b, j, 0))]
    in_specs += [_layer_spec(a.shape, layer) for a in params]
    in_specs += [_const_spec(consts.shape)]
    return pl.pallas_call(
        _mixer_kernel,
        grid=(bsz, s // ts),
        in_specs=in_specs,
        out_specs=pl.BlockSpec((1, ts, D_MODEL), lambda b, j: (b, j, 0)),
        out_shape=jax.ShapeDtypeStruct(h.shape, F32),
        scratch_shapes=[
            pltpu.VMEM((ts, SSD_INNER), F32),
            pltpu.VMEM((CONV_HALO, CONV_DIM), F32),
            pltpu.VMEM((ts, CONV_DIM), F32),
            pltpu.VMEM((ts, POOL_WIDTH), F32),
            pltpu.VMEM((CHUNK + ts, POOL_WIDTH), BF16),
            pltpu.VMEM((SSD_GROUPS, SSD_STATE, GROUP_WIDTH), F32),
            pltpu.VMEM((nch, CHUNK, CHUNK), F32),
            pltpu.VMEM((ts, SSD_INNER), F32),
            pltpu.VMEM((ts, SSD_INNER), BF16),
            pltpu.VMEM((ts, SSD_INNER), BF16),
            pltpu.VMEM((ts, SSD_INNER), BF16),
            pltpu.VMEM((nch, SSD_GROUPS, SSD_STATE, CHUNK), BF16),
            pltpu.VMEM((nch, SSD_GROUPS, CHUNK, SSD_STATE), BF16),
            pltpu.VMEM((nch, CHUNK, CHUNK), F32),
            pltpu.VMEM((ts, SSD_INNER), F32),
            pltpu.VMEM((ts, SSD_INNER), F32),
            pltpu.VMEM((ts, D_MIX), BF16),
            pltpu.VMEM((ts, D_MODEL), F32),
        ],
        compiler_params=pltpu.CompilerParams(
            dimension_semantics=("parallel", "arbitrary"), vmem_limit_bytes=VMEM_LIMIT_BYTES),
        name="mixer",
    )(h, *params, consts)


def _kv_kernel(mem_ref, g_ref, wkv_ref, kt_ref, v_ref):
    mn = _rms(mem_ref[0], g_ref[...]).astype(BF16)
    kv = _dot(mn, wkv_ref[...])
    kt_ref[0] = kv[:, 0:D_MODEL].T.astype(BF16)
    v_ref[0] = kv[:, D_MODEL:2 * D_MODEL].astype(BF16)


def _kv(mem, g, w_kv):
    bsz = mem.shape[0]
    depth = w_kv.shape[0]
    return pl.pallas_call(
        _kv_kernel,
        grid=(depth, bsz),
        in_specs=[pl.BlockSpec((1, MEM_LEN, D_MODEL), lambda l, b: (b, 0, 0)),
                  pl.BlockSpec((None, 1, D_MODEL), lambda l, b: (l, 0, 0)),
                  pl.BlockSpec((None, D_MODEL, 2 * D_MODEL), lambda l, b: (l, 0, 0))],
        out_specs=[pl.BlockSpec((None, 1, D_MODEL, MEM_LEN), lambda l, b: (l, b, 0, 0)),
                   pl.BlockSpec((None, 1, MEM_LEN, D_MODEL), lambda l, b: (l, b, 0, 0))],
        out_shape=[jax.ShapeDtypeStruct((depth, bsz, D_MODEL, MEM_LEN), BF16),
                   jax.ShapeDtypeStruct((depth, bsz, MEM_LEN, D_MODEL), BF16)],
        compiler_params=pltpu.CompilerParams(
            dimension_semantics=("parallel", "parallel"), vmem_limit_bytes=VMEM_LIMIT_BYTES),
        name="memory_kv",
    )(mem, g, w_kv)


def _xattn_ffn_kernel(h_ref, gx_ref, wq_ref, kt_ref, v_ref, wo_ref, gf_ref, wgu_ref, wd_ref, gfin_ref, out_ref,
                      o_s, a_s, h2_s, u2_s, *, final_norm):
    i = pl.program_id(0)

    @pl.when(i == 0)
    def _():
        h2_s[...] = jnp.zeros_like(h2_s)
        u2_s[...] = jnp.zeros_like(u2_s)

    head_cols = [slice(hd * XATTN_HEAD_DIM, (hd + 1) * XATTN_HEAD_DIM) for hd in range(XATTN_HEADS)]
    n_ff = D_FF // FF_COL_CHUNK
    x = h_ref[0]
    xp = h2_s[...]
    up_ = u2_s[...]

    def ffn_dots(ci):
        lo = ci * FF_COL_CHUNK
        return (_dot(up_, wgu_ref[:, lo:lo + FF_COL_CHUNK]),
                _dot(up_, wgu_ref[:, D_FF + lo:D_FF + lo + FF_COL_CHUNK]))

    def ffn_act(ci, gate_up):
        lo = ci * FF_COL_CHUNK
        a_s[:, lo:lo + FF_COL_CHUNK] = (_silu(gate_up[0]) * gate_up[1]).astype(BF16)

    gu = ffn_dots(0)
    u = _rms(x, gx_ref[...]).astype(BF16)
    q = (_dot(u, wq_ref[...]) * (XATTN_HEAD_DIM ** -0.5)).astype(BF16)
    ffn_act(0, gu)
    scores = [_dot(q[:, cols], kt_ref[0, cols, :]) for cols in head_cols]
    for ci in range(1, n_ff):
        gu = ffn_dots(ci)
        if ci == 1:
            ps = []
            for s in scores:
                p = jnp.exp(s - jnp.max(s, axis=-1, keepdims=True))
                ps.append((p.astype(BF16), jnp.sum(p, axis=-1, keepdims=True)))
        ffn_act(ci, gu)
    for cols, (p, den) in zip(head_cols, ps):
        o = _dot(p, v_ref[0, :, cols])
        o_s[:, cols] = (o * (1.0 / den)).astype(BF16)
    x2 = x + _dot(o_s[...], wo_ref[...])
    y = xp + _dot(a_s[...], wd_ref[...])
    h2_s[...] = x2
    u2_s[...] = _rms(x2, gf_ref[...]).astype(BF16)
    if final_norm:
        y = _rms(y, gfin_ref[...])
    out_ref[0] = y


def _xattn_ffn(h, layer, gx, w_q, kt, v, w_o, gf, w_gate_up, w_down, g_final, final_norm):
    bsz, s, _ = h.shape
    tq = SEQ_BLOCK
    nps = s // tq
    n_blocks = bsz * nps

    def cur(i):
        blk = jnp.minimum(i, n_blocks - 1)
        return (blk // nps, blk % nps, 0)

    def prev(i):
        blk = jnp.maximum(i - 1, 0)
        return (blk // nps, blk % nps, 0)

    def mem_of_cur(i):
        return (layer, jnp.minimum(i, n_blocks - 1) // nps, 0, 0)

    return pl.pallas_call(
        functools.partial(_xattn_ffn_kernel, final_norm=final_norm),
        grid=(n_blocks + 1,),
        in_specs=[pl.BlockSpec((1, tq, D_MODEL), cur),
                  _layer_spec(gx.shape, layer), _layer_spec(w_q.shape, layer),
                  pl.BlockSpec((None, 1, D_MODEL, MEM_LEN), mem_of_cur),
                  pl.BlockSpec((None, 1, MEM_LEN, D_MODEL), mem_of_cur),
                  _layer_spec(w_o.shape, layer), _layer_spec(gf.shape, layer),
                  _layer_spec(w_gate_up.shape, layer), _layer_spec(w_down.shape, layer),
                  _const_spec((1, D_MODEL))],
        out_specs=pl.BlockSpec((1, tq, D_MODEL), prev),
        out_shape=jax.ShapeDtypeStruct(h.shape, F32),
        scratch_shapes=[pltpu.VMEM((tq, D_MODEL), BF16), pltpu.VMEM((tq, D_FF), BF16),
                        pltpu.VMEM((tq, D_MODEL), F32), pltpu.VMEM((tq, D_MODEL), BF16)],
        compiler_params=pltpu.CompilerParams(
            dimension_semantics=("arbitrary",), vmem_limit_bytes=VMEM_LIMIT_BYTES),
        name="xattn_swiglu_final" if final_norm else "xattn_swiglu",
    )(h, gx, w_q, kt, v, w_o, gf, w_gate_up, w_down, g_final.reshape(1, -1))


def kernel(x, mem, mix_norm, w_in, conv_w, conv_b, dt_bias, a_log, d_skip, ssd_norm, pool_w, pool_scale,
           w_out_mix, xattn_norm, mem_norm, w_q, w_kv, w_o, ffn_norm, w_gate_up, w_down, final_norm):
    depth = w_in.shape[0]
    row = lambda a: a.reshape(depth, 1, -1)
    mixer_params = _prep_mixer_params(mix_norm, w_in, conv_w, conv_b, dt_bias, a_log, d_skip, ssd_norm, pool_w,
                                      pool_scale, w_out_mix)
    xattn_g, mem_g, ffn_g = row(xattn_norm), row(mem_norm), row(ffn_norm)
    wq_b, wkv_b, wo_b = w_q.astype(BF16), w_kv.astype(BF16), w_o.astype(BF16)
    wgu_b, wd_b = w_gate_up.astype(BF16), w_down.astype(BF16)
    kt, v = _kv(mem, mem_g, wkv_b)
    h = x
    for l in range(depth):
        h = _mixer(h, l, mixer_params)
        h = _xattn_ffn(h, l, xattn_g, wq_b, kt, v, wo_b, ffn_g, wgu_b, wd_b, final_norm,
                       final_norm=(l == depth - 1))
    return h
```
